```python
import math
import jax, jax.numpy as jnp
from jax import lax
import numpy as np

D_MODEL = 2048
BATCH = 4
SEQ = 2048
DEPTH = 4
DEC_BATCH = 8
DEC_SEQ = 1
PAST_LEN = 16384
PAGE_SIZE = 128

N_META = 16
HEAD_DIM = 128
N_HEADS = D_MODEL // (2 * HEAD_DIM)
ATT_Q_W = 2 * N_HEADS * HEAD_DIM
ATT_K_W = 2 * N_HEADS * HEAD_DIM
ATT_V_W = N_HEADS * 2 * HEAD_DIM
ATT_OUT_W = ATT_V_W
ATT_SCALE = HEAD_DIM ** -0.5
Q_BLOCK = 128
POOL_WINDOWS = (2, 4, 8, 16)
POOL_GROUPS = len(POOL_WINDOWS)
POOL_W = D_MODEL // 2
POOL_GW = POOL_W // POOL_GROUPS
POOL_HIST = max(POOL_WINDOWS) - 1
CONV_W = D_MODEL // 2
CONV_K = 3
CONV_HIST = CONV_K - 1
N_BRANCH = 3
OFF_Q = 0
OFF_K = OFF_Q + ATT_Q_W
OFF_V = OFF_K + ATT_K_W
OFF_POOL = OFF_V + ATT_V_W
OFF_CB = OFF_POOL + POOL_W
OFF_CC = OFF_CB + CONV_W
OFF_CX = OFF_CC + CONV_W
OFF_G = OFF_CX + CONV_W
IN_W = OFF_G + N_BRANCH * D_MODEL
D_FF = 256 * ((8 * D_MODEL // 3 + 255) // 256)
N_EXPERTS = 8
TOP_K = 2
N_DENSE = (DEPTH + 1) // 2
N_MOE = DEPTH // 2
RMS_EPS = 1e-6
SUBLN_EPS = 1e-5
NEG_INF = -1e30

kernel_name = 'hybrid_pool_diffattn_shortconv_step'


def _lambda_init(l):
    return 0.8 - 0.6 * math.exp(-0.3 * l)


def _rmsnorm(x, g, eps=RMS_EPS):
    xf = x.astype(jnp.float32)
    y = xf * lax.rsqrt(jnp.mean(xf * xf, axis=-1, keepdims=True) + eps)
    return (y * g.astype(jnp.float32)).astype(x.dtype)


def _diff_attn(q, k, v, q_pos, lam):
    B, Tq = q.shape[:2]
    Tk = k.shape[1]
    qh = q.reshape(B, Tq, N_HEADS, 2, HEAD_DIM)
    kh = k.reshape(B, Tk, N_HEADS, 2, HEAD_DIM)
    s = jnp.einsum('bqhcd,bkhcd->bhcqk', qh, kh, preferred_element_type=jnp.float32) * ATT_SCALE
    mask = jnp.arange(Tk)[None, :] <= q_pos[:, None]
    s = jnp.where(mask, s, NEG_INF)
    pr = jax.nn.softmax(s, axis=-1)
    a = pr[:, :, 0] - lam * pr[:, :, 1]
    return jnp.einsum('bhqk,bkhe->bqhe', a.astype(v.dtype), v)


def _attend_prompt(q, k, v, lam):
    B, T = q.shape[:2]
    n_blk = -(-T // Q_BLOCK)
    pad = n_blk * Q_BLOCK - T
    padw = ((0, 0), (0, pad), (0, 0), (0, 0))
    q = jnp.pad(q, padw)
    k = jnp.pad(k, padw)
    v = jnp.pad(v, padw)
    qb = q.reshape(B, n_blk, Q_BLOCK, 2 * N_HEADS, HEAD_DIM).transpose(1, 0, 2, 3, 4)
    starts = jnp.arange(n_blk) * Q_BLOCK

    def blk(args):
        q_blk, s0 = args
        return _diff_attn(q_blk, k, v, s0 + jnp.arange(Q_BLOCK), lam)

    o = lax.map(blk, (qb, starts))
    o = o.transpose(1, 0, 2, 3, 4).reshape(B, n_blk * Q_BLOCK, N_HEADS, 2 * HEAD_DIM)
    return o[:, :T]


def _pool_mix(u, hist, pos0, w_grp, scale):
    B, T, C = u.shape
    cat = jnp.concatenate([hist, u], axis=1)
    cs = jnp.cumsum(cat.astype(jnp.float32), axis=1)
    cs = jnp.concatenate([jnp.zeros((B, 1, C), jnp.float32), cs], axis=1)
    pos = pos0 + jnp.arange(T)
    hi = cs[:, POOL_HIST + 1:POOL_HIST + 1 + T]
    means = []
    for g, w in enumerate(POOL_WINDOWS):
        c0, c1 = g * POOL_GW, (g + 1) * POOL_GW
        lo = cs[:, POOL_HIST + 1 - w:POOL_HIST + 1 - w + T, c0:c1]
        cnt = jnp.minimum(pos + 1, w).astype(jnp.float32)[None, :, None]
        means.append((hi[..., c0:c1] - lo) / cnt)
    pooled = (jnp.concatenate(means, axis=-1) - u.astype(jnp.float32)).astype(u.dtype)
    mixed = jnp.einsum('btgc,gcd->btgd', pooled.reshape(B, T, POOL_GROUPS, POOL_GW), w_grp)
    return mixed.reshape(B, T, C) * scale, cat[:, -POOL_HIST:]


def _conv_mix(cb, cc, cx, hist, w):
    u = cc * cx
    cat = jnp.concatenate([hist, u], axis=1)
    z = lax.conv_general_dilated(cat, w[:, None, :], window_strides=(1,), padding='VALID',
                                 dimension_numbers=('NWC', 'WIO', 'NWC'),
                                 feature_group_count=CONV_W)
    return cb * z, cat[:, -CONV_HIST:]


def _swiglu(h, w1, w3, w2):
    return (jax.nn.silu(h @ w1) * (h @ w3)) @ w2


def _moe(h, router, w1, w3, w2):
    shp = h.shape
    t = h.reshape(-1, D_MODEL)
    logits = jnp.dot(t, router, preferred_element_type=jnp.float32)
    top_v, top_i = lax.top_k(logits, TOP_K)
    top_w = jax.nn.softmax(top_v, axis=-1)
    gate = jnp.sum(jax.nn.one_hot(top_i, N_EXPERTS, dtype=jnp.float32) * top_w[..., None], axis=1)
    out = jnp.zeros(t.shape, jnp.float32)
    for e in range(N_EXPERTS):
        out = out + gate[:, e:e + 1] * _swiglu(t, w1[e], w3[e], w2[e]).astype(jnp.float32)
    return out.astype(h.dtype).reshape(shp)


def _layer(x, l, p, pos0, past_k, past_v, pool_hist, conv_hist):
    B, T, _ = x.shape
    h = _rmsnorm(x, p['norm_mix'][l])
    z = h @ p['w_in'][l]
    q = z[..., OFF_Q:OFF_K].reshape(B, T, 2 * N_HEADS, HEAD_DIM)
    k = z[..., OFF_K:OFF_V].reshape(B, T, 2 * N_HEADS, HEAD_DIM)
    v = z[..., OFF_V:OFF_POOL].reshape(B, T, N_HEADS, 2 * HEAD_DIM)
    pu = z[..., OFF_POOL:OFF_CB]
    cb = z[..., OFF_CB:OFF_CC]
    cc = z[..., OFF_CC:OFF_CX]
    cx = z[..., OFF_CX:OFF_G]
    gates = jax.nn.sigmoid(z[..., OFF_G:].reshape(B, T, N_BRANCH, D_MODEL))

    lam_init = _lambda_init(l)
    f32 = jnp.float32
    lam = (jnp.exp(jnp.sum(p['lambda_q1'][l].astype(f32) * p['lambda_k1'][l].astype(f32)))
           - jnp.exp(jnp.sum(p['lambda_q2'][l].astype(f32) * p['lambda_k2'][l].astype(f32)))
           + lam_init)
    if past_k is None:
        o = _attend_prompt(q, k, v, lam)
    else:
        kk = jnp.concatenate([past_k, k], axis=1)
        vv = jnp.concatenate([past_v, v], axis=1)
        o = _diff_attn(q, kk, vv, pos0 + jnp.arange(T), lam)
    o = _rmsnorm(o, p['subln_g'][l], SUBLN_EPS) * (1.0 - lam_init)
    br_attn = o.reshape(B, T, ATT_OUT_W) @ p['w_up_attn'][l]

    pm, pool_new = _pool_mix(pu, pool_hist, pos0, p['pool_w'][l], p['pool_scale'][l])
    br_pool = pm @ p['w_up_pool'][l]

    cm, conv_new = _conv_mix(cb, cc, cx, conv_hist, p['conv_w'][l])
    br_conv = cm @ p['w_up_conv'][l]

    merged = gates[..., 0, :] * br_pool + gates[..., 1, :] * br_attn + gates[..., 2, :] * br_conv
    x = x + merged @ p['w_o'][l]

    h = _rmsnorm(x, p['norm_ffn'][l])
    if l % 2 == 0:
        i = l // 2
        f = _swiglu(h, p['ffn_w1'][i], p['ffn_w3'][i], p['ffn_w2'][i])
    else:
        i = l // 2
        f = _moe(h, p['router_w'][i], p['moe_w1'][i], p['moe_w3'][i], p['moe_w2'][i])
    return x + f, k, v, pool_new, conv_new


def setup_inputs(seed: int = 0) -> dict:
    key = jax.random.key(seed)
    ks = jax.random.split(key, 32)
    f32 = jnp.float32
    n_pages = PAST_LEN // PAGE_SIZE
    n_used = DEC_BATCH * n_pages
    n_pool = n_used + max(1, n_used // 4)

    def nrm(k, shape, s):
        return jax.random.normal(k, shape, f32) * s

    page_table = jax.random.permutation(ks[6], n_pool)[:n_used].reshape(DEC_BATCH, n_pages).astype(jnp.int32)
    return {
        'x_prompt': nrm(ks[0], (BATCH, SEQ, D_MODEL), 1.0),
        'x_sample': nrm(ks[1], (DEC_BATCH, DEC_SEQ, D_MODEL), 1.0),
        'cache_k': nrm(ks[2], (n_pool, DEPTH, PAGE_SIZE, 2 * N_HEADS, HEAD_DIM), 1.0),
        'cache_v': nrm(ks[3], (n_pool, DEPTH, PAGE_SIZE, N_HEADS, 2 * HEAD_DIM), 1.0),
        'state_pool': nrm(ks[4], (DEPTH, DEC_BATCH, POOL_HIST, POOL_W), 1.0),
        'state_conv': nrm(ks[5], (DEPTH, DEC_BATCH, CONV_HIST, CONV_W), 1.0),
        'page_table': page_table,
        'meta_tokens': nrm(ks[7], (N_META, D_MODEL), 1.0),
        'norm_mix': 1.0 + nrm(ks[8], (DEPTH, D_MODEL), 0.02),
        'w_in': nrm(ks[9], (DEPTH, D_MODEL, IN_W), D_MODEL ** -0.5),
        'lambda_q1': nrm(ks[10], (DEPTH, HEAD_DIM), 0.1),
        'lambda_k1': nrm(ks[11], (DEPTH, HEAD_DIM), 0.1),
        'lambda_q2': nrm(ks[12], (DEPTH, HEAD_DIM), 0.1),
        'lambda_k2': nrm(ks[13], (DEPTH, HEAD_DIM), 0.1),
        'subln_g': 1.0 + nrm(ks[14], (DEPTH, 2 * HEAD_DIM), 0.02),
        'pool_w': nrm(ks[15], (DEPTH, POOL_GROUPS, POOL_GW, POOL_GW), POOL_GW ** -0.5),
        'pool_scale': 1.0 + nrm(ks[16], (DEPTH, POOL_W), 0.1),
        'conv_w': nrm(ks[17], (DEPTH, CONV_K, CONV_W), CONV_K ** -0.5),
        'w_up_pool': nrm(ks[18], (DEPTH, POOL_W, D_MODEL), POOL_W ** -0.5),
        'w_up_attn': nrm(ks[19], (DEPTH, ATT_OUT_W, D_MODEL), ATT_OUT_W ** -0.5),
        'w_up_conv': nrm(ks[20], (DEPTH, CONV_W, D_MODEL), CONV_W ** -0.5),
        'w_o': nrm(ks[21], (DEPTH, D_MODEL, D_MODEL), D_MODEL ** -0.5),
        'norm_ffn': 1.0 + nrm(ks[22], (DEPTH, D_MODEL), 0.02),
        'ffn_w1': nrm(ks[23], (N_DENSE, D_MODEL, D_FF), D_MODEL ** -0.5),
        'ffn_w3': nrm(ks[24], (N_DENSE, D_MODEL, D_FF), D_MODEL ** -0.5),
        'ffn_w2': nrm(ks[25], (N_DENSE, D_FF, D_MODEL), D_FF ** -0.5),
        'router_w': nrm(ks[26], (N_MOE, D_MODEL, N_EXPERTS), D_MODEL ** -0.5),
        'moe_w1': nrm(ks[27], (N_MOE, N_EXPERTS, D_MODEL, D_FF), D_MODEL ** -0.5),
        'moe_w3': nrm(ks[28], (N_MOE, N_EXPERTS, D_MODEL, D_FF), D_MODEL ** -0.5),
        'moe_w2': nrm(ks[29], (N_MOE, N_EXPERTS, D_FF, D_MODEL), D_FF ** -0.5),
        'norm_final': 1.0 + nrm(ks[30], (D_MODEL,), 0.02),
    }


def reference(x_prompt, x_sample, cache_k, cache_v, state_pool, state_conv, page_table,
              meta_tokens, norm_mix, w_in, lambda_q1, lambda_k1, lambda_q2, lambda_k2, subln_g,
              pool_w, pool_scale, conv_w, w_up_pool, w_up_attn, w_up_conv, w_o, norm_ffn,
              ffn_w1, ffn_w3, ffn_w2, router_w, moe_w1, moe_w3, moe_w2, norm_final):
    p = {
        'norm_mix': norm_mix, 'w_in': w_in,
        'lambda_q1': lambda_q1, 'lambda_k1': lambda_k1, 'lambda_q2': lambda_q2, 'lambda_k2': lambda_k2,
        'subln_g': subln_g, 'pool_w': pool_w, 'pool_scale': pool_scale, 'conv_w': conv_w,
        'w_up_pool': w_up_pool, 'w_up_attn': w_up_attn, 'w_up_conv': w_up_conv, 'w_o': w_o,
        'norm_ffn': norm_ffn, 'ffn_w1': ffn_w1, 'ffn_w3': ffn_w3, 'ffn_w2': ffn_w2,
        'router_w': router_w, 'moe_w1': moe_w1, 'moe_w3': moe_w3, 'moe_w2': moe_w2,
    }
    dt = x_prompt.dtype
    bp = x_prompt.shape[0]
    bs = x_sample.shape[0]
    n_pages = PAST_LEN // PAGE_SIZE
    past_len = n_pages * PAGE_SIZE

    meta = jnp.broadcast_to(meta_tokens.astype(dt)[None], (bp, N_META, D_MODEL))
    xp = jnp.concatenate([meta, x_prompt], axis=1)
    xs = x_sample
    zero_pool = jnp.zeros((bp, POOL_HIST, POOL_W), dt)
    zero_conv = jnp.zeros((bp, CONV_HIST, CONV_W), dt)

    kp_l, vp_l, pp_l, cp_l = [], [], [], []
    ks_l, vs_l, ps_l, cs_l = [], [], [], []
    for l in range(DEPTH):
        xp, kp, vp, pp, cp = _layer(xp, l, p, 0, None, None, zero_pool, zero_conv)
        past_k = cache_k[page_table, l].reshape(bs, past_len, 2 * N_HEADS, HEAD_DIM)
        past_v = cache_v[page_table, l].reshape(bs, past_len, N_HEADS, 2 * HEAD_DIM)
        xs, k_s, v_s, p_s, c_s = _layer(xs, l, p, past_len, past_k, past_v, state_pool[l], state_conv[l])
        kp_l.append(kp); vp_l.append(vp); pp_l.append(pp); cp_l.append(cp)
        ks_l.append(k_s); vs_l.append(v_s); ps_l.append(p_s); cs_l.append(c_s)

    y_prompt = _rmsnorm(xp[:, N_META:], norm_final)
    y_sample = _rmsnorm(xs, norm_final)
    return (y_prompt, y_sample,
            jnp.stack(kp_l, axis=1), jnp.stack(vp_l, axis=1),
            jnp.stack(pp_l, axis=0), jnp.stack(cp_l, axis=0),
            jnp.stack(ks_l, axis=1), jnp.stack(vs_l, axis=1),
            jnp.stack(ps_l, axis=0), jnp.stack(cs_l, axis=0))
```

```python
import functools
import math

import jax
import jax.numpy as jnp
from jax import lax
from jax.experimental import pallas as pl
from jax.experimental.pallas import tpu as pltpu

F32 = jnp.float32
BF16 = jnp.bfloat16

POOL_WINDOWS = (2, 4, 8, 16)
TOP_K = 2
RMS_EPS = 1e-6
SUBLN_EPS = 1e-5
NEG_INF = -1e30

V7X_LANES = 128
V7X_VMEM_LIMIT_BYTES = 56 * 1024 * 1024

ROW_TILE_TARGET = 1040
NORM_TILE_TARGET = 416
SEQ_TILE_TARGET = 344
SAMPLE_PAGES_PER_STEP = 4


def _lambda_init(l):
    return 0.8 - 0.6 * math.exp(-0.3 * l)


def _pick_tile(n, target, mult=8):
    best = None
    for t in range(mult, min(n, target) + 1, mult):
        if n % t == 0:
            best = t
    assert best is not None, (n, target, mult)
    return best


def _cparams(n_axes):
    return pltpu.CompilerParams(dimension_semantics=("arbitrary",) * n_axes,
                                vmem_limit_bytes=V7X_VMEM_LIMIT_BYTES)


def _norm_kernel(*refs, has_add, n_experts, eps):
    refs = list(refs)
    x_ref = refs.pop(0)
    f_ref = refs.pop(0) if has_add else None
    g_ref = refs.pop(0)
    r_ref = refs.pop(0) if n_experts else None
    xo_ref = refs.pop(0) if has_add else None
    h_ref = refs.pop(0)
    gate_ref = refs.pop(0) if n_experts else None

    x = x_ref[...]
    if has_add:
        x = x + f_ref[...]
        xo_ref[...] = x
    ms = jnp.mean(x * x, axis=-1, keepdims=True)
    y = x * lax.rsqrt(ms + eps) * g_ref[...]
    h_ref[...] = y.astype(h_ref.dtype)
    if n_experts:
        logits = jnp.dot(y.astype(BF16), r_ref[...].astype(BF16), preferred_element_type=F32)
        lane = lax.broadcasted_iota(jnp.int32, logits.shape, 1)
        logits = jnp.where(lane < n_experts, logits, NEG_INF)
        v1 = jnp.max(logits, axis=-1, keepdims=True)
        i1 = jnp.min(jnp.where(logits == v1, lane, V7X_LANES), axis=-1, keepdims=True)
        rest = jnp.where(lane == i1, NEG_INF, logits)
        v2 = jnp.max(rest, axis=-1, keepdims=True)
        i2 = jnp.min(jnp.where(rest == v2, lane, V7X_LANES), axis=-1, keepdims=True)
        e2 = jnp.exp(v2 - v1)
        w1 = 1.0 / (1.0 + e2)
        w2 = e2 / (1.0 + e2)
        gate_ref[...] = jnp.where(lane == 0, i1.astype(F32), jnp.where(lane == 1, i2.astype(F32),
                                  jnp.where(lane == 2, w1, jnp.where(lane == 3, w2, 0.0))))


def _norm(x, f, g, router, *, tm, out_dtype, eps=RMS_EPS):
    m, d = x.shape
    has_add = f is not None
    n_experts = 0 if router is None else router.shape[1]
    row = pl.BlockSpec((tm, d), lambda i: (i, 0))
    ins, in_specs = [x], [row]
    if has_add:
        ins.append(f)
        in_specs.append(row)
    ins.append(g.reshape(1, d))
    in_specs.append(pl.BlockSpec((1, d), lambda i: (0, 0)))
    if n_experts:
        ins.append(jnp.pad(router, ((0, 0), (0, V7X_LANES - n_experts))))
        in_specs.append(pl.BlockSpec((d, V7X_LANES), lambda i: (0, 0)))
    out_shape, out_specs = [], []
    if has_add:
        out_shape.append(jax.ShapeDtypeStruct((m, d), F32))
        out_specs.append(row)
    out_shape.append(jax.ShapeDtypeStruct((m, d), out_dtype))
    out_specs.append(row)
    if n_experts:
        out_shape.append(jax.ShapeDtypeStruct((m, V7X_LANES), F32))
        out_specs.append(pl.BlockSpec((tm, V7X_LANES), lambda i: (i, 0)))
    outs = pl.pallas_call(
        functools.partial(_norm_kernel, has_add=has_add, n_experts=n_experts, eps=eps),
        grid=(m // tm,), in_specs=in_specs, out_specs=out_specs, out_shape=out_shape,
        compiler_params=_cparams(1), name="rmsnorm")(*ins)
    outs = list(outs)
    x_new = outs.pop(0) if has_add else x
    h = outs.pop(0)
    gate = outs.pop(0) if n_experts else None
    return x_new, h, gate


def _mm_kernel(*refs, has_res):
    if has_res:
        x_ref, w_ref, r_ref, o_ref, w_sc = refs
    else:
        x_ref, w_ref, o_ref, w_sc = refs

    @pl.when(pl.program_id(1) == 0)
    def _():
        w_sc[...] = w_ref[...].astype(BF16)

    acc = jnp.dot(x_ref[...], w_sc[...], preferred_element_type=F32)
    if has_res:
        acc = r_ref[...] + acc
    o_ref[...] = acc.astype(o_ref.dtype)


def _matmul(x, w_stack, layer, res, *, tm, tn, out_dtype):
    m, k = x.shape
    n = w_stack.shape[2]
    ins = [x, w_stack]
    in_specs = [pl.BlockSpec((tm, k), lambda j, i: (i, 0)),
                pl.BlockSpec((None, k, tn), lambda j, i: (layer, 0, j))]
    if res is not None:
        ins.append(res)
        in_specs.append(pl.BlockSpec((tm, tn), lambda j, i: (i, j)))
    return pl.pallas_call(
        functools.partial(_mm_kernel, has_res=res is not None),
        grid=(n // tn, m // tm), in_specs=in_specs,
        out_specs=pl.BlockSpec((tm, tn), lambda j, i: (i, j)),
        out_shape=jax.ShapeDtypeStruct((m, n), out_dtype),
        scratch_shapes=[pltpu.VMEM((k, tn), BF16)],
        compiler_params=_cparams(2), name="matmul")(*ins)


def _diff_lambda(lq1_ref, lk1_ref, lq2_ref, lk2_ref, lam_init):
    a = jnp.sum(lq1_ref[...] * lk1_ref[...], axis=-1, keepdims=True)
    b = jnp.sum(lq2_ref[...] * lk2_ref[...], axis=-1, keepdims=True)
    return jnp.exp(a) - jnp.exp(b) + lam_init


def _subln(o, g, lam_init):
    ms = jnp.mean(o * o, axis=-1, keepdims=True)
    return o * lax.rsqrt(ms + SUBLN_EPS) * g * (1.0 - lam_init)


def _attn_prompt_kernel(q_ref, k_ref, v_ref, lq1_ref, lk1_ref, lq2_ref, lk2_ref, g_ref, o_ref,
                        m_sc, l_sc, acc_sc, *, tq, hd, scale, lam_init):
    qi = pl.program_id(2)
    q = q_ref[...]
    qs = [q[:, :hd].astype(BF16), q[:, hd:].astype(BF16)]
    m_sc[...] = jnp.full(m_sc.shape, NEG_INF, F32)
    l_sc[...] = jnp.zeros(l_sc.shape, F32)
    acc_sc[...] = jnp.zeros(acc_sc.shape, F32)
    row = lax.broadcasted_iota(jnp.int32, (tq, tq), 0)
    col = lax.broadcasted_iota(jnp.int32, (tq, tq), 1)

    def chunk(c, on_diagonal):
        start = pl.multiple_of(c * tq, 8)
        k = k_ref[pl.ds(start, tq), :].astype(BF16)
        v = v_ref[pl.ds(start, tq), :].astype(BF16)
        for i in range(2):
            s = lax.dot_general(qs[i], k[:, i * hd:(i + 1) * hd], (((1,), (1,)), ((), ())),
                                preferred_element_type=F32) * scale
            if on_diagonal:
                s = jnp.where(col <= row, s, NEG_INF)
            m_old = m_sc[i]
            m_new = jnp.maximum(m_old, jnp.max(s, axis=-1, keepdims=True))
            alpha = jnp.exp(m_old - m_new)
            p = jnp.exp(s - m_new)
            l_sc[i] = alpha * l_sc[i] + jnp.sum(p, axis=-1, keepdims=True)
            acc_sc[i] = alpha * acc_sc[i] + jnp.dot(p.astype(BF16), v, preferred_element_type=F32)
            m_sc[i] = m_new

    def body(c, carry):
        chunk(c, False)
        return carry

    lax.fori_loop(0, qi, body, 0)
    chunk(qi, True)
    lam = _diff_lambda(lq1_ref, lk1_ref, lq2_ref, lk2_ref, lam_init)
    o = acc_sc[0] / l_sc[0] - lam * (acc_sc[1] / l_sc[1])
    o_ref[...] = _subln(o, g_ref[...], lam_init).astype(o_ref.dtype)


def _lambda_specs(n_grid_axes, layer, hd):
    if n_grid_axes == 3:
        return [pl.BlockSpec((None, 1, hd), lambda b, h, i: (layer, 0, 0))] * 4
    return [pl.BlockSpec((None, 1, hd), lambda b, j, pt: (layer, 0, 0))] * 4


def _attn_prompt(z, lams, subln_g, layer, *, bp, t, n_heads, hd, off_q, off_k, off_v, tq):
    nq = t // tq
    hw = 2 * hd
    return pl.pallas_call(
        functools.partial(_attn_prompt_kernel, tq=tq, hd=hd, scale=hd ** -0.5, lam_init=_lambda_init(layer)),
        grid=(bp, n_heads, nq),
        in_specs=[pl.BlockSpec((tq, hw), lambda b, h, i: (b * nq + i, off_q // hw + h)),
                  pl.BlockSpec((t, hw), lambda b, h, i: (b, off_k // hw + h)),
                  pl.BlockSpec((t, hw), lambda b, h, i: (b, off_v // hw + h)),
                  *_lambda_specs(3, layer, hd),
                  pl.BlockSpec((None, 1, hw), lambda b, h, i: (layer, 0, 0))],
        out_specs=pl.BlockSpec((tq, hw), lambda b, h, i: (b * nq + i, h)),
        out_shape=jax.ShapeDtypeStruct((bp * t, n_heads * hw), BF16),
        scratch_shapes=[pltpu.VMEM((2, tq, 1), F32), pltpu.VMEM((2, tq, 1), F32), pltpu.VMEM((2, tq, hw), F32)],
        compiler_params=_cparams(3), name="attn_prompt")(z, z, z, *lams, subln_g)


def _attn_sample_kernel(pt_ref, q_ref, kn_ref, vn_ref, lq1_ref, lk1_ref, lq2_ref, lk2_ref, g_ref, *rest,
                        n_pages_step, n_heads, hd, scale, lam_init):
    k_refs = rest[:n_pages_step]
    v_refs = rest[n_pages_step:2 * n_pages_step]
    o_ref, m_sc, l_sc, acc_sc, rows_sc = rest[2 * n_pages_step:]
    b = pl.program_id(0)
    j = pl.program_id(1)
    last_b = pl.num_programs(0) - 1
    last_j = pl.num_programs(1) - 1
    page = k_refs[0].shape[0]
    hw = 2 * hd

    @pl.when((b == 0) & (j == 0))
    def _():
        rows_sc[...] = jnp.zeros(rows_sc.shape, F32)

    @pl.when(j == 0)
    def _():
        m_sc[...] = jnp.full(m_sc.shape, NEG_INF, F32)
        l_sc[...] = jnp.zeros(l_sc.shape, F32)
        acc_sc[...] = jnp.zeros(acc_sc.shape, F32)

    qs = [q_ref[pl.ds(c, n_heads, stride=2), :].astype(BF16) for c in range(2)]
    lane_head = lax.broadcasted_iota(jnp.int32, (n_heads, page * n_heads), 1) % n_heads
    own_head = lane_head == lax.broadcasted_iota(jnp.int32, (n_heads, page * n_heads), 0)

    scores = []
    for i in range(n_pages_step):
        per_map = []
        for c in range(2):
            kc = k_refs[i][:, pl.ds(c, n_heads, stride=2), :].reshape(page * n_heads, hd).astype(BF16)
            s = lax.dot_general(qs[c], kc, (((1,), (1,)), ((), ())), preferred_element_type=F32) * scale
            per_map.append(jnp.where(own_head, s, NEG_INF))
        scores.append(per_map)

    m_new, alpha = [], []
    for c in range(2):
        m_cur = m_sc[c]
        for i in range(n_pages_step):
            m_cur = jnp.maximum(m_cur, jnp.max(scores[i][c], axis=-1, keepdims=True))
        m_new.append(m_cur)
        alpha.append(jnp.exp(m_sc[c] - m_cur))
        m_sc[c] = m_cur

    l_add = [jnp.zeros((n_heads, 1), F32)] * 2
    acc_add = jnp.zeros((2 * n_heads, hw), F32)
    for i in range(n_pages_step):
        p = [jnp.exp(scores[i][c] - m_new[c]) for c in range(2)]
        l_add = [l_add[c] + jnp.sum(p[c], axis=-1, keepdims=True) for c in range(2)]
        v = v_refs[i][...].reshape(page * n_heads, hw).astype(BF16)
        acc_add = acc_add + jnp.dot(jnp.concatenate(p, axis=0).astype(BF16), v, preferred_element_type=F32)
    for c in range(2):
        l_sc[c] = alpha[c] * l_sc[c] + l_add[c]
        acc_sc[c] = alpha[c] * acc_sc[c] + acc_add[c * n_heads:(c + 1) * n_heads]

    @pl.when(j == last_j)
    def _():
        vn = vn_ref[...].astype(BF16).astype(F32)
        outs = []
        for c in range(2):
            kn = kn_ref[pl.ds(c, n_heads, stride=2), :].astype(BF16).astype(F32)
            s = jnp.sum(qs[c].astype(F32) * kn, axis=-1, keepdims=True) * scale
            m_old = m_sc[c]
            m_fin = jnp.maximum(m_old, s)
            a = jnp.exp(m_old - m_fin)
            p = jnp.exp(s - m_fin)
            l_fin = a * l_sc[c] + p
            p_r = p.astype(BF16).astype(F32)
            outs.append((a * acc_sc[c] + p_r * vn) / l_fin)
        lam = _diff_lambda(lq1_ref, lk1_ref, lq2_ref, lk2_ref, lam_init)
        y = _subln(outs[0] - lam * outs[1], g_ref[...], lam_init)
        for h in range(n_heads):
            rows_sc[pl.ds(b, 1), h * hw:(h + 1) * hw] = y[h:h + 1, :]

    @pl.when((b == last_b) & (j == last_j))
    def _():
        o_ref[...] = rows_sc[...].astype(o_ref.dtype)


def _attn_sample(q_s, k_s, v_s, cache_k, cache_v, page_table, lams, subln_g, layer, *, ts, n_heads, hd):
    bs, n_pages = page_table.shape
    page = cache_k.shape[2]
    hw = 2 * hd
    pps = _pick_tile(n_pages, SAMPLE_PAGES_PER_STEP, 1)

    def page_spec(i, width_dims):
        return pl.BlockSpec((None, None, page) + width_dims,
                            lambda b, j, pt: (pt[b, j * pps + i], layer, 0, 0, 0))

    in_specs = [pl.BlockSpec((None, 2 * n_heads, hd), lambda b, j, pt: (b, 0, 0)),
                pl.BlockSpec((None, 2 * n_heads, hd), lambda b, j, pt: (b, 0, 0)),
                pl.BlockSpec((None, n_heads, hw), lambda b, j, pt: (b, 0, 0)),
                *_lambda_specs(2, layer, hd),
                pl.BlockSpec((None, 1, hw), lambda b, j, pt: (layer, 0, 0))]
    in_specs += [page_spec(i, (2 * n_heads, hd)) for i in range(pps)]
    in_specs += [page_spec(i, (n_heads, hw)) for i in range(pps)]
    ins = [page_table, q_s, k_s, v_s, *lams, subln_g] + [cache_k] * pps + [cache_v] * pps
    grid_spec = pltpu.PrefetchScalarGridSpec(
        num_scalar_prefetch=1, grid=(bs, n_pages // pps), in_specs=in_specs,
        out_specs=pl.BlockSpec((ts, n_heads * hw), lambda b, j, pt: (0, 0)),
        scratch_shapes=[pltpu.VMEM((2, n_heads, 1), F32), pltpu.VMEM((2, n_heads, 1), F32),
                        pltpu.VMEM((2, n_heads, hw), F32), pltpu.VMEM((ts, n_heads * hw), F32)])
    return pl.pallas_call(
        functools.partial(_attn_sample_kernel, n_pages_step=pps, n_heads=n_heads, hd=hd, scale=hd ** -0.5,
                          lam_init=_lambda_init(layer)),
        grid_spec=grid_spec, out_shape=jax.ShapeDtypeStruct((ts, n_heads * hw), BF16),
        compiler_params=_cparams(2), name="attn_sample")(*ins)


def _shift_rows(x, k, t_idx):
    return jnp.where(t_idx >= k, pltpu.roll(x, k, 0), 0.0)


def _pool_prompt_kernel(u_ref, w_ref, sc_ref, o_ref, *, gw):
    t = u_ref.shape[0]
    t_idx = lax.broadcasted_iota(jnp.int32, (t, 1), 0)
    for g, win in enumerate(POOL_WINDOWS):
        cols = slice(g * gw, (g + 1) * gw)
        u = u_ref[:, cols]
        s = u
        k = 1
        while k < win:
            s = s + _shift_rows(s, k, t_idx)
            k *= 2
        cnt = jnp.minimum(t_idx + 1, win).astype(F32)
        pooled = (s / cnt - u).astype(BF16)
        mixed = jnp.dot(pooled, w_ref[g].astype(BF16), preferred_element_type=F32)
        o_ref[:, cols] = (mixed * sc_ref[:, cols]).astype(o_ref.dtype)


def _pool_prompt(z, pool_w, pool_scale, layer, *, bp, t, off_pool):
    n_g, gw = pool_w.shape[1], pool_w.shape[2]
    pw = n_g * gw
    return pl.pallas_call(
        functools.partial(_pool_prompt_kernel, gw=gw),
        grid=(bp,),
        in_specs=[pl.BlockSpec((t, pw), lambda b: (b, off_pool // pw)),
                  pl.BlockSpec((None, n_g, gw, gw), lambda b: (layer, 0, 0, 0)),
                  pl.BlockSpec((None, 1, pw), lambda b: (layer, 0, 0))],
        out_specs=pl.BlockSpec((t, pw), lambda b: (b, 0)),
        out_shape=jax.ShapeDtypeStruct((bp * t, pw), BF16),
        compiler_params=_cparams(1), name="pool_prompt")(z, pool_w, pool_scale)


def _conv_prompt_kernel(cb_ref, cc_ref, cx_ref, w_ref, o_ref, tail_ref):
    t = cb_ref.shape[0]
    t_idx = lax.broadcasted_iota(jnp.int32, (t, 1), 0)
    u = cc_ref[...] * cx_ref[...]
    w = w_ref[...]
    zc = w[0:1] * _shift_rows(u, 2, t_idx) + w[1:2] * _shift_rows(u, 1, t_idx) + w[2:3] * u
    o_ref[...] = (cb_ref[...] * zc).astype(o_ref.dtype)
    tail_ref[...] = u[t - 8:, :]


def _conv_prompt(z, conv_w, layer, *, bp, t, off_cb, off_cc, off_cx):
    cw = conv_w.shape[2]
    cbk = _pick_tile(cw, 256, V7X_LANES)
    ncb = cw // cbk

    def zspec(off):
        return pl.BlockSpec((t, cbk), lambda b, c: (b, off // cbk + c))

    return pl.pallas_call(
        _conv_prompt_kernel, grid=(bp, ncb),
        in_specs=[zspec(off_cb), zspec(off_cc), zspec(off_cx),
                  pl.BlockSpec((None, conv_w.shape[1], cbk), lambda b, c: (layer, 0, c))],
        out_specs=[pl.BlockSpec((t, cbk), lambda b, c: (b, c)),
                   pl.BlockSpec((None, 8, cbk), lambda b, c: (b, 0, c))],
        out_shape=[jax.ShapeDtypeStruct((bp * t, cw), BF16), jax.ShapeDtypeStruct((bp, 8, cw), F32)],
        compiler_params=_cparams(2), name="conv_prompt")(z, z, z, conv_w)


def _mix_sample_kernel(pu_ref, cb_ref, cc_ref, cx_ref, hp_ref, hc_ref, pw_ref, ps_ref, cw_ref,
                       pm_ref, cm_ref, cu_ref, *, bs, gw, past_len):
    n_hist = hp_ref.shape[0]
    pm_ref[...] = jnp.zeros(pm_ref.shape, pm_ref.dtype)
    cm_ref[...] = jnp.zeros(cm_ref.shape, cm_ref.dtype)
    for g, win in enumerate(POOL_WINDOWS):
        cols = slice(g * gw, (g + 1) * gw)
        u = pu_ref[:, cols]
        s = u
        for i in range(1, win):
            s = s + hp_ref[n_hist - i][:, cols]
        cnt = float(min(past_len + 1, win))
        pooled = (s / cnt - u).astype(BF16)
        mixed = jnp.dot(pooled, pw_ref[g].astype(BF16), preferred_element_type=F32)
        pm_ref[0:bs, cols] = (mixed * ps_ref[:, cols]).astype(pm_ref.dtype)
    u = cc_ref[...] * cx_ref[...]
    w = cw_ref[...]
    zc = w[0:1] * hc_ref[0] + w[1:2] * hc_ref[1] + w[2:3] * u
    cm_ref[0:bs, :] = (cb_ref[...] * zc).astype(cm_ref.dtype)
    cu_ref[...] = u


def _mix_sample(z, hist_pool, hist_conv, pool_w, pool_scale, conv_w, layer, *,
                row0, ts, bs, off_pool, off_cb, off_cc, off_cx, past_len):
    n_g, gw = pool_w.shape[1], pool_w.shape[2]
    pw = n_g * gw
    cw = conv_w.shape[2]

    def zspec(off, width):
        return pl.BlockSpec((bs, width), lambda i: (row0 // bs, off // width))

    return pl.pallas_call(
        functools.partial(_mix_sample_kernel, bs=bs, gw=gw, past_len=past_len),
        grid=(1,),
        in_specs=[zspec(off_pool, pw), zspec(off_cb, cw), zspec(off_cc, cw), zspec(off_cx, cw),
                  pl.BlockSpec(hist_pool.shape, lambda i: (0, 0, 0)),
                  pl.BlockSpec(hist_conv.shape, lambda i: (0, 0, 0)),
                  pl.BlockSpec((None, n_g, gw, gw), lambda i: (layer, 0, 0, 0)),
                  pl.BlockSpec((None, 1, pw), lambda i: (layer, 0, 0)),
                  pl.BlockSpec((None, conv_w.shape[1], cw), lambda i: (layer, 0, 0))],
        out_specs=[pl.BlockSpec((ts, pw), lambda i: (0, 0)),
                   pl.BlockSpec((ts, cw), lambda i: (0, 0)),
                   pl.BlockSpec((bs, cw), lambda i: (0, 0))],
        out_shape=[jax.ShapeDtypeStruct((ts, pw), BF16), jax.ShapeDtypeStruct((ts, cw), BF16),
                   jax.ShapeDtypeStruct((bs, cw), F32)],
        compiler_params=_cparams(1), name="mix_sample")(
            z, z, z, z, hist_pool, hist_conv, pool_w, pool_scale, conv_w)


def _merge_kernel(pm_ref, oa_ref, cm_ref, g0_ref, g1_ref, g2_ref, wp_ref, wa_ref, wc_ref, o_ref,
                  wp_sc, wa_sc, wc_sc):
    @pl.when(pl.program_id(1) == 0)
    def _():
        wp_sc[...] = wp_ref[...].astype(BF16)
        wa_sc[...] = wa_ref[...].astype(BF16)
        wc_sc[...] = wc_ref[...].astype(BF16)

    br_pool = jnp.dot(pm_ref[...], wp_sc[...], preferred_element_type=F32)
    br_attn = jnp.dot(oa_ref[...], wa_sc[...], preferred_element_type=F32)
    br_conv = jnp.dot(cm_ref[...], wc_sc[...], preferred_element_type=F32)
    merged = (jax.nn.sigmoid(g0_ref[...]) * br_pool + jax.nn.sigmoid(g1_ref[...]) * br_attn
              + jax.nn.sigmoid(g2_ref[...]) * br_conv)
    o_ref[...] = merged.astype(o_ref.dtype)


def _merge(pm, oa, cm, z, w_up_pool, w_up_attn, w_up_conv, layer, *, off_g, d, tm, tn):
    m = pm.shape[0]
    pw, aw, cw = pm.shape[1], oa.shape[1], cm.shape[1]

    def act(width):
        return pl.BlockSpec((tm, width), lambda j, i: (i, 0))

    def gate(k):
        return pl.BlockSpec((tm, tn), lambda j, i: (i, (off_g + k * d) // tn + j))

    def wspec(width):
        return pl.BlockSpec((None, width, tn), lambda j, i: (layer, 0, j))

    return pl.pallas_call(
        _merge_kernel, grid=(d // tn, m // tm),
        in_specs=[act(pw), act(aw), act(cw), gate(0), gate(1), gate(2), wspec(pw), wspec(aw), wspec(cw)],
        out_specs=pl.BlockSpec((tm, tn), lambda j, i: (i, j)),
        out_shape=jax.ShapeDtypeStruct((m, d), BF16),
        scratch_shapes=[pltpu.VMEM((pw, tn), BF16), pltpu.VMEM((aw, tn), BF16), pltpu.VMEM((cw, tn), BF16)],
        compiler_params=_cparams(2), name="merge")(pm, oa, cm, z, z, z, w_up_pool, w_up_attn, w_up_conv)


def _swiglu_step(h, w1_ref, w3_ref, w2_ref):
    a = jnp.dot(h, w1_ref[...].astype(BF16), preferred_element_type=F32)
    b = jnp.dot(h, w3_ref[...].astype(BF16), preferred_element_type=F32)
    act = (a * jax.nn.sigmoid(a)) * b
    return jnp.dot(act.astype(BF16), w2_ref[...].astype(BF16), preferred_element_type=F32)


def _ffn_kernel(h_ref, w1_ref, w3_ref, w2_ref, o_ref):
    @pl.when(pl.program_id(1) == 0)
    def _():
        o_ref[...] = jnp.zeros(o_ref.shape, o_ref.dtype)

    o_ref[...] += _swiglu_step(h_ref[...], w1_ref, w3_ref, w2_ref)


def _ffn(h, w1, w3, w2, idx, *, tm, tf):
    m, d = h.shape
    d_ff = w1.shape[2]
    up = pl.BlockSpec((None, d, tf), lambda i, f: (idx, 0, f))
    down = pl.BlockSpec((None, tf, d), lambda i, f: (idx, f, 0))
    return pl.pallas_call(
        _ffn_kernel, grid=(m // tm, d_ff // tf),
        in_specs=[pl.BlockSpec((tm, d), lambda i, f: (i, 0)), up, up, down],
        out_specs=pl.BlockSpec((tm, d), lambda i, f: (i, 0)),
        out_shape=jax.ShapeDtypeStruct((m, d), F32),
        compiler_params=_cparams(2), name="ffn")(h, w1, w3, w2)


def _route_tables(route, n_valid, n_experts, tile_rows):
    m = route.shape[0]
    idx = route[:, :TOP_K].astype(jnp.int32)
    valid = jnp.arange(m) < n_valid
    wts = jnp.where(valid[:, None], route[:, TOP_K:2 * TOP_K], 0.0)
    onehot = ((idx[:, :, None] == jnp.arange(n_experts)) & valid[:, None, None]).astype(jnp.int32)
    onehot = onehot.reshape(m * TOP_K, n_experts)
    csum = jnp.cumsum(onehot, axis=0)
    rank = jnp.sum((csum - onehot) * onehot, axis=1)
    tiles_e = (csum[-1] + tile_rows - 1) // tile_rows
    tile_end = jnp.cumsum(tiles_e)
    start_row = (tile_end - tiles_e) * tile_rows
    valid2 = jnp.repeat(valid, TOP_K)
    pos = jnp.where(valid2, start_row[idx.reshape(-1)] + rank, 0)
    n_tiles = (n_valid * TOP_K) // tile_rows + n_experts
    n_rows = n_tiles * tile_rows
    tok = jnp.repeat(jnp.arange(m, dtype=jnp.int32), TOP_K)
    src = jnp.zeros((n_rows,), jnp.int32).at[jnp.where(valid2, pos, n_rows)].set(tok, mode="drop")
    n_used = tile_end[-1]
    tile_expert = jnp.searchsorted(tile_end, jnp.minimum(jnp.arange(n_tiles), n_used - 1), side="right")
    tile_expert = jnp.clip(tile_expert, 0, n_experts - 1).astype(jnp.int32)
    return (src.reshape(n_tiles, 1, tile_rows), tile_expert, n_used.reshape(1).astype(jnp.int32),
            pos.astype(jnp.int32), wts)


def _row_copy(src_hbm, src_row, dst_vmem, dst_row, sem):
    return pltpu.make_async_copy(src_hbm.at[pl.ds(src_row, 1), :], dst_vmem.at[pl.ds(dst_row, 1), :], sem)


def _moe_group_kernel(te_ref, nu_ref, src_ref, h_hbm, w1_ref, w3_ref, w2_ref, o_ref, x32_sc, x16_sc, sem):
    del te_ref
    i = pl.program_id(0)
    f = pl.program_id(1)
    used = i < nu_ref[0]
    rows = x32_sc.shape[0]

    @pl.when(f == 0)
    def _():
        o_ref[...] = jnp.zeros(o_ref.shape, o_ref.dtype)

    @pl.when(used & (f == 0))
    def _():
        def issue(r, carry):
            _row_copy(h_hbm, src_ref[0, r], x32_sc, r, sem).start()
            return carry

        def wait(r, carry):
            _row_copy(h_hbm, 0, x32_sc, r, sem).wait()
            return carry

        lax.fori_loop(0, rows, issue, 0)
        lax.fori_loop(0, rows, wait, 0)
        x16_sc[...] = x32_sc[...].astype(BF16)

    @pl.when(used)
    def _():
        o_ref[...] += _swiglu_step(x16_sc[...], w1_ref, w3_ref, w2_ref)


def _moe_group(h32, src, tile_expert, n_used, w1, w3, w2, idx, *, tf):
    n_tiles, _, rows = src.shape
    d = h32.shape[1]
    d_ff = w1.shape[3]
    nf = d_ff // tf

    def f_blk(i, f, nu):
        return jnp.where(i < nu[0], f, nf - 1)

    up = pl.BlockSpec((None, None, d, tf), lambda i, f, te, nu: (idx, te[i], 0, f_blk(i, f, nu)))
    down = pl.BlockSpec((None, None, tf, d), lambda i, f, te, nu: (idx, te[i], f_blk(i, f, nu), 0))
    grid_spec = pltpu.PrefetchScalarGridSpec(
        num_scalar_prefetch=2, grid=(n_tiles, nf),
        in_specs=[pl.BlockSpec((None, 1, rows), lambda i, f, te, nu: (i, 0, 0), memory_space=pltpu.SMEM),
                  pl.BlockSpec(memory_space=pl.ANY), up, up, down],
        out_specs=pl.BlockSpec((rows, d), lambda i, f, te, nu: (i, 0)),
        scratch_shapes=[pltpu.VMEM((rows, d), F32), pltpu.VMEM((rows, d), BF16), pltpu.SemaphoreType.DMA(())])
    return pl.pallas_call(
        _moe_group_kernel, grid_spec=grid_spec,
        out_shape=jax.ShapeDtypeStruct((n_tiles * rows, d), F32),
        compiler_params=_cparams(2), name="moe_group")(tile_expert, n_used, src, h32, w1, w3, w2)


def _moe_combine_kernel(pos_ref, w_ref, y_hbm, o_ref, y_sc, sem):
    rows = o_ref.shape[0]

    def issue(r, carry):
        for s in range(TOP_K):
            _row_copy(y_hbm, pos_ref[0, TOP_K * r + s], y_sc.at[s], r, sem).start()
        return carry

    def wait(r, carry):
        for s in range(TOP_K):
            _row_copy(y_hbm, 0, y_sc.at[s], r, sem).wait()
        return carry

    lax.fori_loop(0, rows, issue, 0)
    lax.fori_loop(0, rows, wait, 0)
    w = w_ref[...]
    acc = w[:, 0:1] * y_sc[0]
    for s in range(1, TOP_K):
        acc = acc + w[:, s:s + 1] * y_sc[s]
    o_ref[...] = acc


def _moe_combine(y, pos, wts, *, tm):
    m = wts.shape[0]
    d = y.shape[1]
    return pl.pallas_call(
        _moe_combine_kernel, grid=(m // tm,),
        in_specs=[pl.BlockSpec((None, 1, TOP_K * tm), lambda i: (i, 0, 0), memory_space=pltpu.SMEM),
                  pl.BlockSpec((tm, TOP_K), lambda i: (i, 0)),
                  pl.BlockSpec(memory_space=pl.ANY)],
        out_specs=pl.BlockSpec((tm, d), lambda i: (i, 0)),
        out_shape=jax.ShapeDtypeStruct((m, d), F32),
        scratch_shapes=[pltpu.VMEM((TOP_K, tm, d), F32), pltpu.SemaphoreType.DMA(())],
        compiler_params=_cparams(1), name="moe_combine")(pos.reshape(m // tm, 1, TOP_K * tm), wts, y)


def _moe(h32, route, w1, w3, w2, idx, *, n_valid, tile_rows, tf, tm_combine):
    src, tile_expert, n_used, pos, wts = _route_tables(route, n_valid, w1.shape[1], tile_rows)
    y = _moe_group(h32, src, tile_expert, n_used, w1, w3, w2, idx, tf=tf)
    return _moe_combine(y, pos, wts, tm=tm_combine)


def kernel(x_prompt, x_sample, cache_k, cache_v, state_pool, state_conv, page_table, meta_tokens, norm_mix, w_in, lambda_q1, lambda_k1, lambda_q2, lambda_k2, subln_g, pool_w, pool_scale, conv_w, w_up_pool, w_up_attn, w_up_conv, w_o, norm_ffn, ffn_w1, ffn_w3, ffn_w2, router_w, moe_w1, moe_w3, moe_w2, norm_final):
    bp, seq, d = x_prompt.shape
    bs, dec_seq, _ = x_sample.shape
    assert dec_seq == 1 and bs % 8 == 0
    depth = w_in.shape[0]
    n_meta = meta_tokens.shape[0]
    t = seq + n_meta
    hd = cache_k.shape[-1]
    n_heads = cache_v.shape[-2]
    hw = 2 * hd
    att_w = n_heads * hw
    pw = state_pool.shape[-1]
    cw = state_conv.shape[-1]
    page = cache_k.shape[2]
    past_len = page_table.shape[1] * page
    off_q, off_k, off_v = 0, att_w, 2 * att_w
    off_pool = 3 * att_w
    off_cb = off_pool + pw
    off_cc, off_cx, off_g = off_cb + cw, off_cb + 2 * cw, off_cb + 3 * cw
    in_w = off_g + 3 * d
    assert w_in.shape[2] == in_w and state_pool.shape[2] == max(POOL_WINDOWS) - 1

    n_prompt = bp * t
    ts = next(c for c in (64, 32, 16, 8) if n_prompt % c == 0 and c >= bs)
    m_pad = n_prompt + ts
    tm = _pick_tile(m_pad, ROW_TILE_TARGET)
    tq = _pick_tile(t, SEQ_TILE_TARGET)
    tn_in = _pick_tile(in_w, 1024, V7X_LANES)
    tn_d = _pick_tile(d, 512, V7X_LANES)
    tf = _pick_tile(ffn_w1.shape[2], 256, V7X_LANES)
    tm_merge = _pick_tile(m_pad, ROW_TILE_TARGET // 2)
    tm_norm = _pick_tile(m_pad, NORM_TILE_TARGET)
    assert n_prompt % bs == 0 and all(o % w == 0 for o, w in
                                      ((off_k, hw), (off_v, hw), (off_pool, pw), (off_cb, cw), (off_g, tn_d)))

    dt = x_prompt.dtype
    meta = jnp.broadcast_to(meta_tokens.astype(dt)[None], (bp, n_meta, d))
    x = jnp.concatenate([jnp.concatenate([meta, x_prompt], axis=1).reshape(n_prompt, d),
                         x_sample.reshape(bs, d), jnp.zeros((ts - bs, d), dt)], axis=0)

    lam_vecs = [v.reshape(depth, 1, hd) for v in (lambda_q1, lambda_k1, lambda_q2, lambda_k2)]
    subln = subln_g.reshape(depth, 1, hw)
    pool_sc = pool_scale.reshape(depth, 1, pw)

    kp_l, vp_l, pp_l, cp_l, ks_l, vs_l, ps_l, cs_l = ([] for _ in range(8))
    f = None
    for l in range(depth):
        x, h, _ = _norm(x, f, norm_mix[l], None, tm=tm_norm, out_dtype=BF16)
        z = _matmul(h, w_in, l, None, tm=tm, tn=tn_in, out_dtype=F32)

        zs = z[n_prompt:n_prompt + bs]
        q_s = zs[:, off_q:off_k].reshape(bs, 2 * n_heads, hd)
        k_s = zs[:, off_k:off_v].reshape(bs, 2 * n_heads, hd)
        v_s = zs[:, off_v:off_pool].reshape(bs, n_heads, hw)
        zp = z[:n_prompt].reshape(bp, t, in_w)

        oa_p = _attn_prompt(z, lam_vecs, subln, l, bp=bp, t=t, n_heads=n_heads, hd=hd,
                            off_q=off_q, off_k=off_k, off_v=off_v, tq=tq)
        oa_s = _attn_sample(q_s, k_s, v_s, cache_k, cache_v, page_table, lam_vecs, subln, l,
                            ts=ts, n_heads=n_heads, hd=hd)
        pm_p = _pool_prompt(z, pool_w, pool_sc, l, bp=bp, t=t, off_pool=off_pool)
        cm_p, conv_tail = _conv_prompt(z, conv_w, l, bp=bp, t=t, off_cb=off_cb, off_cc=off_cc, off_cx=off_cx)
        pm_s, cm_s, cu_s = _mix_sample(z, jnp.swapaxes(state_pool[l], 0, 1), jnp.swapaxes(state_conv[l], 0, 1),
                                       pool_w, pool_sc, conv_w, l, row0=n_prompt, ts=ts, bs=bs, off_pool=off_pool,
                                       off_cb=off_cb, off_cc=off_cc, off_cx=off_cx, past_len=past_len)
        oa = jnp.concatenate([oa_p, oa_s], axis=0)
        pm = jnp.concatenate([pm_p, pm_s], axis=0)
        cm = jnp.concatenate([cm_p, cm_s], axis=0)
        merged = _merge(pm, oa, cm, z, w_up_pool, w_up_attn, w_up_conv, l, off_g=off_g, d=d,
                        tm=tm_merge, tn=tn_d)
        x = _matmul(merged, w_o, l, x, tm=tm, tn=tn_d, out_dtype=F32)

        if l % 2 == 0:
            _, h2, _ = _norm(x, None, norm_ffn[l], None, tm=tm_norm, out_dtype=BF16)
            f = _ffn(h2, ffn_w1, ffn_w3, ffn_w2, l // 2, tm=tm, tf=tf)
        else:
            _, h2, route = _norm(x, None, norm_ffn[l], router_w[l // 2], tm=tm_norm, out_dtype=F32)
            f = _moe(h2, route, moe_w1, moe_w3, moe_w2, l // 2, n_valid=n_prompt + bs, tile_rows=tm, tf=tf,
                     tm_combine=tm_merge)

        kp_l.append(zp[:, :, off_k:off_v].reshape(bp, t, 2 * n_heads, hd))
        vp_l.append(zp[:, :, off_v:off_pool].reshape(bp, t, n_heads, hw))
        pp_l.append(zp[:, t - state_pool.shape[2]:, off_pool:off_cb])
        cp_l.append(conv_tail[:, 8 - state_conv.shape[2]:])
        ks_l.append(k_s.reshape(bs, 1, 2 * n_heads, hd))
        vs_l.append(v_s.reshape(bs, 1, n_heads, hw))
        ps_l.append(jnp.concatenate([state_pool[l][:, 1:], zs[:, None, off_pool:off_cb]], axis=1))
        cs_l.append(jnp.concatenate([state_conv[l][:, 1:], cu_s[:, None]], axis=1))

    _, y, _ = _norm(x, f, norm_final, None, tm=tm_norm, out_dtype=F32)
    y_prompt = y[:n_prompt].reshape(bp, t, d)[:, n_meta:]
    y_sample = y[n_prompt:n_prompt + bs].reshape(bs, 1, d)
    return (y_prompt, y_sample,
            jnp.stack(kp_l, axis=1), jnp.stack(vp_l, axis=1), jnp.stack(pp_l, axis=0), jnp.stack(cp_l, axis=0),
            jnp.stack(ks_l, axis=1), jnp.stack(vs_l, axis=1), jnp.stack(ps_l, axis=0), jnp.stack(cs_l, axis=0))
```

```python
import functools
import math

import jax
import jax.numpy as jnp
from jax import lax
from jax.experimental import pallas as pl
from jax.experimental.pallas import tpu as pltpu

F32 = jnp.float32
BF16 = jnp.bfloat16

POOL_WINDOWS = (2, 4, 8, 16)
TOP_K = 2
RMS_EPS = 1e-6
SUBLN_EPS = 1e-5
NEG_INF = -1e30

V7X_LANES = 128
V7X_VMEM_LIMIT_BYTES = 56 * 1024 * 1024

ROW_TILE_TARGET = 1040
NORM_TILE_TARGET = 416
SEQ_TILE_TARGET = 344
SAMPLE_PAGES_PER_STEP = 4


def _lambda_init(l):
    return 0.8 - 0.6 * math.exp(-0.3 * l)


def _pick_tile(n, target, mult=8):
    best = None
    for t in range(mult, min(n, target) + 1, mult):
        if n % t == 0:
            best = t
    assert best is not None, (n, target, mult)
    return best


def _cparams(n_axes):
    return pltpu.CompilerParams(dimension_semantics=("arbitrary",) * n_axes,
                                vmem_limit_bytes=V7X_VMEM_LIMIT_BYTES)


def _norm_kernel(*refs, has_add, n_experts, eps):
    refs = list(refs)
    x_ref = refs.pop(0)
    f_ref = refs.pop(0) if has_add else None
    g_ref = refs.pop(0)
    r_ref = refs.pop(0) if n_experts else None
    xo_ref = refs.pop(0) if has_add else None
    h_ref = refs.pop(0)
    gate_ref = refs.pop(0) if n_experts else None

    x = x_ref[...]
    if has_add:
        x = x + f_ref[...]
        xo_ref[...] = x
    ms = jnp.mean(x * x, axis=-1, keepdims=True)
    y = x * lax.rsqrt(ms + eps) * g_ref[...]
    h_ref[...] = y.astype(h_ref.dtype)
    if n_experts:
        logits = jnp.dot(y.astype(BF16), r_ref[...].astype(BF16), preferred_element_type=F32)
        lane = lax.broadcasted_iota(jnp.int32, logits.shape, 1)
        logits = jnp.where(lane < n_experts, logits, NEG_INF)
        v1 = jnp.max(logits, axis=-1, keepdims=True)
        i1 = jnp.min(jnp.where(logits == v1, lane, V7X_LANES), axis=-1, keepdims=True)
        rest = jnp.where(lane == i1, NEG_INF, logits)
        v2 = jnp.max(rest, axis=-1, keepdims=True)
        i2 = jnp.min(jnp.where(rest == v2, lane, V7X_LANES), axis=-1, keepdims=True)
        e2 = jnp.exp(v2 - v1)
        w1 = 1.0 / (1.0 + e2)
        w2 = e2 / (1.0 + e2)
        gate_ref[...] = jnp.where(lane == 0, i1.astype(F32), jnp.where(lane == 1, i2.astype(F32),
                                  jnp.where(lane == 2, w1, jnp.where(lane == 3, w2, 0.0))))


def _norm(x, f, g, router, *, tm, out_dtype, eps=RMS_EPS):
    m, d = x.shape
    has_add = f is not None
    n_experts = 0 if router is None else router.shape[1]
    row = pl.BlockSpec((tm, d), lambda i: (i, 0))
    ins, in_specs = [x], [row]
    if has_add:
        ins.append(f)
        in_specs.append(row)
    ins.append(g.reshape(1, d))
    in_specs.append(pl.BlockSpec((1, d), lambda i: (0, 0)))
    if n_experts:
        ins.append(jnp.pad(router, ((0, 0), (0, V7X_LANES - n_experts))))
        in_specs.append(pl.BlockSpec((d, V7X_LANES), lambda i: (0, 0)))
    out_shape, out_specs = [], []
    if has_add:
        out_shape.append(jax.ShapeDtypeStruct((m, d), F32))
        out_specs.append(row)
    out_shape.append(jax.ShapeDtypeStruct((m, d), out_dtype))
    out_specs.append(row)
    if n_experts:
        out_shape.append(jax.ShapeDtypeStruct((m, V7X_LANES), F32))
        out_specs.append(pl.BlockSpec((tm, V7X_LANES), lambda i: (i, 0)))
    outs = pl.pallas_call(
        functools.partial(_norm_kernel, has_add=has_add, n_experts=n_experts, eps=eps),
        grid=(m // tm,), in_specs=in_specs, out_specs=out_specs, out_shape=out_shape,
        compiler_params=_cparams(1), name="rmsnorm")(*ins)
    outs = list(outs)
    x_new = outs.pop(0) if has_add else x
    h = outs.pop(0)
    gate = outs.pop(0) if n_experts else None
    return x_new, h, gate


def _mm_kernel(*refs, has_res):
    if has_res:
        x_ref, w_ref, r_ref, o_ref, w_sc = refs
    else:
        x_ref, w_ref, o_ref, w_sc = refs

    @pl.when(pl.program_id(1) == 0)
    def _():
        w_sc[...] = w_ref[...].astype(BF16)

    acc = jnp.dot(x_ref[...], w_sc[...], preferred_element_type=F32)
    if has_res:
        acc = r_ref[...] + acc
    o_ref[...] = acc.astype(o_ref.dtype)


def _matmul(x, w_stack, layer, res, *, tm, tn, out_dtype):
    m, k = x.shape
    n = w_stack.shape[2]
    ins = [x, w_stack]
    in_specs = [pl.BlockSpec((tm, k), lambda j, i: (i, 0)),
                pl.BlockSpec((None, k, tn), lambda j, i: (layer, 0, j))]
    if res is not None:
        ins.append(res)
        in_specs.append(pl.BlockSpec((tm, tn), lambda j, i: (i, j)))
    return pl.pallas_call(
        functools.partial(_mm_kernel, has_res=res is not None),
        grid=(n // tn, m // tm), in_specs=in_specs,
        out_specs=pl.BlockSpec((tm, tn), lambda j, i: (i, j)),
        out_shape=jax.ShapeDtypeStruct((m, n), out_dtype),
        scratch_shapes=[pltpu.VMEM((k, tn), BF16)],
        compiler_params=_cparams(2), name="matmul")(*ins)


def _diff_lambda(lq1_ref, lk1_ref, lq2_ref, lk2_ref, lam_init):
    a = jnp.sum(lq1_ref[...] * lk1_ref[...], axis=-1, keepdims=True)
    b = jnp.sum(lq2_ref[...] * lk2_ref[...], axis=-1, keepdims=True)
    return jnp.exp(a) - jnp.exp(b) + lam_init


def _subln(o, g, lam_init):
    ms = jnp.mean(o * o, axis=-1, keepdims=True)
    return o * lax.rsqrt(ms + SUBLN_EPS) * g * (1.0 - lam_init)


def _attn_prompt_kernel(q_ref, k_ref, v_ref, lq1_ref, lk1_ref, lq2_ref, lk2_ref, g_ref, o_ref,
                        s_sc, red_sc, acc_sc, *, tq, hd, scale, lam_init):
    qi = pl.program_id(2)
    q = q_ref[...]
    qs = [q[:, :hd].astype(BF16), q[:, hd:].astype(BF16)]
    row = lax.broadcasted_iota(jnp.int32, (tq, tq), 0)
    col = lax.broadcasted_iota(jnp.int32, (tq, tq), 1)

    def scores(c, i):
        start = pl.multiple_of(c * tq, 8)
        k = k_ref[pl.ds(start, tq), i * hd:(i + 1) * hd].astype(BF16)
        return lax.dot_general(qs[i], k, (((1,), (1,)), ((), ())), preferred_element_type=F32) * scale

    for i in range(2):
        s = jnp.where(col <= row, scores(qi, i), NEG_INF)
        s_sc[i, qi] = s
        red_sc[i] = s

    def pass1(c, carry):
        for i in range(2):
            s = scores(c, i)
            s_sc[i, c] = s
            red_sc[i] = jnp.maximum(red_sc[i], s)
        return carry

    lax.fori_loop(0, qi, pass1, 0)

    for i in range(2):
        m = jnp.max(red_sc[i], axis=-1, keepdims=True)
        s_sc[i, qi] = s_sc[i, qi] - m
        red_sc[2 + i] = jnp.broadcast_to(m, (tq, tq))

    def probs(c, i, on_diagonal):
        s = s_sc[i, c]
        return jnp.exp(s if on_diagonal else s - red_sc[2 + i])

    v_d = v_ref[pl.ds(pl.multiple_of(qi * tq, 8), tq), :].astype(BF16)
    for i in range(2):
        p = probs(qi, i, True)
        red_sc[i] = p
        acc_sc[i] = jnp.dot(p.astype(BF16), v_d, preferred_element_type=F32)

    def pass2(c, carry):
        v = v_ref[pl.ds(pl.multiple_of(c * tq, 8), tq), :].astype(BF16)
        for i in range(2):
            p = probs(c, i, False)
            red_sc[i] = red_sc[i] + p
            acc_sc[i] = acc_sc[i] + jnp.dot(p.astype(BF16), v, preferred_element_type=F32)
        return carry

    lax.fori_loop(0, qi, pass2, 0)

    lam = _diff_lambda(lq1_ref, lk1_ref, lq2_ref, lk2_ref, lam_init)
    outs = [acc_sc[i] / jnp.sum(red_sc[i], axis=-1, keepdims=True) for i in range(2)]
    o_ref[...] = _subln(outs[0] - lam * outs[1], g_ref[...], lam_init).astype(o_ref.dtype)


def _lambda_specs(n_grid_axes, layer, hd):
    if n_grid_axes == 3:
        return [pl.BlockSpec((None, 1, hd), lambda b, h, i: (layer, 0, 0))] * 4
    return [pl.BlockSpec((None, 1, hd), lambda b, j, pt: (layer, 0, 0))] * 4


def _attn_prompt(z, lams, subln_g, layer, *, bp, t, n_heads, hd, off_q, off_k, off_v, tq):
    nq = t // tq
    hw = 2 * hd
    return pl.pallas_call(
        functools.partial(_attn_prompt_kernel, tq=tq, hd=hd, scale=hd ** -0.5, lam_init=_lambda_init(layer)),
        grid=(bp, n_heads, nq),
        in_specs=[pl.BlockSpec((tq, hw), lambda b, h, i: (b * nq + i, off_q // hw + h)),
                  pl.BlockSpec((t, hw), lambda b, h, i: (b, off_k // hw + h)),
                  pl.BlockSpec((t, hw), lambda b, h, i: (b, off_v // hw + h)),
                  *_lambda_specs(3, layer, hd),
                  pl.BlockSpec((None, 1, hw), lambda b, h, i: (layer, 0, 0))],
        out_specs=pl.BlockSpec((tq, hw), lambda b, h, i: (b * nq + i, h)),
        out_shape=jax.ShapeDtypeStruct((bp * t, n_heads * hw), BF16),
        scratch_shapes=[pltpu.VMEM((2, nq, tq, tq), F32), pltpu.VMEM((4, tq, tq), F32), pltpu.VMEM((2, tq, hw), F32)],
        compiler_params=_cparams(3), name="attn_prompt")(z, z, z, *lams, subln_g)


def _attn_sample_kernel(pt_ref, q_ref, kn_ref, vn_ref, lq1_ref, lk1_ref, lq2_ref, lk2_ref, g_ref, *rest,
                        n_pages_step, n_heads, hd, scale, lam_init):
    k_refs = rest[:n_pages_step]
    v_refs = rest[n_pages_step:2 * n_pages_step]
    o_ref, m_sc, l_sc, acc_sc, rows_sc = rest[2 * n_pages_step:]
    b = pl.program_id(0)
    j = pl.program_id(1)
    last_b = pl.num_programs(0) - 1
    last_j = pl.num_programs(1) - 1
    page = k_refs[0].shape[0]
    hw = 2 * hd

    @pl.when((b == 0) & (j == 0))
    def _():
        rows_sc[...] = jnp.zeros(rows_sc.shape, F32)

    @pl.when(j == 0)
    def _():
        m_sc[...] = jnp.full(m_sc.shape, NEG_INF, F32)
        l_sc[...] = jnp.zeros(l_sc.shape, F32)
        acc_sc[...] = jnp.zeros(acc_sc.shape, F32)

    qs = [q_ref[pl.ds(c, n_heads, stride=2), :].astype(BF16) for c in range(2)]
    lane_head = lax.broadcasted_iota(jnp.int32, (n_heads, page * n_heads), 1) % n_heads
    own_head = lane_head == lax.broadcasted_iota(jnp.int32, (n_heads, page * n_heads), 0)

    scores = []
    for i in range(n_pages_step):
        per_map = []
        for c in range(2):
            kc = k_refs[i][:, pl.ds(c, n_heads, stride=2), :].reshape(page * n_heads, hd).astype(BF16)
            s = lax.dot_general(qs[c], kc, (((1,), (1,)), ((), ())), preferred_element_type=F32) * scale
            per_map.append(jnp.where(own_head, s, NEG_INF))
        scores.append(per_map)

    m_new, alpha = [], []
    for c in range(2):
        m_cur = m_sc[c]
        for i in range(n_pages_step):
            m_cur = jnp.maximum(m_cur, jnp.max(scores[i][c], axis=-1, keepdims=True))
        m_new.append(m_cur)
        alpha.append(jnp.exp(m_sc[c] - m_cur))
        m_sc[c] = m_cur

    l_add = [jnp.zeros((n_heads, 1), F32)] * 2
    acc_add = jnp.zeros((2 * n_heads, hw), F32)
    for i in range(n_pages_step):
        p = [jnp.exp(scores[i][c] - m_new[c]) for c in range(2)]
        l_add = [l_add[c] + jnp.sum(p[c], axis=-1, keepdims=True) for c in range(2)]
        v = v_refs[i][...].reshape(page * n_heads, hw).astype(BF16)
        acc_add = acc_add + jnp.dot(jnp.concatenate(p, axis=0).astype(BF16), v, preferred_element_type=F32)
    for c in range(2):
        l_sc[c] = alpha[c] * l_sc[c] + l_add[c]
        acc_sc[c] = alpha[c] * acc_sc[c] + acc_add[c * n_heads:(c + 1) * n_heads]

    @pl.when(j == last_j)
    def _():
        vn = vn_ref[...].astype(BF16).astype(F32)
        outs = []
        for c in range(2):
            kn = kn_ref[pl.ds(c, n_heads, stride=2), :].astype(BF16).astype(F32)
            s = jnp.sum(qs[c].astype(F32) * kn, axis=-1, keepdims=True) * scale
            m_old = m_sc[c]
            m_fin = jnp.maximum(m_old, s)
            a = jnp.exp(m_old - m_fin)
            p = jnp.exp(s - m_fin)
            l_fin = a * l_sc[c] + p
            p_r = p.astype(BF16).astype(F32)
            outs.append((a * acc_sc[c] + p_r * vn) / l_fin)
        lam = _diff_lambda(lq1_ref, lk1_ref, lq2_ref, lk2_ref, lam_init)
        y = _subln(outs[0] - lam * outs[1], g_ref[...], lam_init)
        for h in range(n_heads):
            rows_sc[pl.ds(b, 1), h * hw:(h + 1) * hw] = y[h:h + 1, :]

    @pl.when((b == last_b) & (j == last_j))
    def _():
        o_ref[...] = rows_sc[...].astype(o_ref.dtype)


def _attn_sample(q_s, k_s, v_s, cache_k, cache_v, page_table, lams, subln_g, layer, *, ts, n_heads, hd):
    bs, n_pages = page_table.shape
    page = cache_k.shape[2]
    hw = 2 * hd
    pps = _pick_tile(n_pages, SAMPLE_PAGES_PER_STEP, 1)

    def page_spec(i, width_dims):
        return pl.BlockSpec((None, None, page) + width_dims,
                            lambda b, j, pt: (pt[b, j * pps + i], layer, 0, 0, 0))

    in_specs = [pl.BlockSpec((None, 2 * n_heads, hd), lambda b, j, pt: (b, 0, 0)),
                pl.BlockSpec((None, 2 * n_heads, hd), lambda b, j, pt: (b, 0, 0)),
                pl.BlockSpec((None, n_heads, hw), lambda b, j, pt: (b, 0, 0)),
                *_lambda_specs(2, layer, hd),
                pl.BlockSpec((None, 1, hw), lambda b, j, pt: (layer, 0, 0))]
    in_specs += [page_spec(i, (2 * n_heads, hd)) for i in range(pps)]
    in_specs += [page_spec(i, (n_heads, hw)) for i in range(pps)]
    ins = [page_table, q_s, k_s, v_s, *lams, subln_g] + [cache_k] * pps + [cache_v] * pps
    grid_spec = pltpu.PrefetchScalarGridSpec(
        num_scalar_prefetch=1, grid=(bs, n_pages // pps), in_specs=in_specs,
        out_specs=pl.BlockSpec((ts, n_heads * hw), lambda b, j, pt: (0, 0)),
        scratch_shapes=[pltpu.VMEM((2, n_heads, 1), F32), pltpu.VMEM((2, n_heads, 1), F32),
                        pltpu.VMEM((2, n_heads, hw), F32), pltpu.VMEM((ts, n_heads * hw), F32)])
    return pl.pallas_call(
        functools.partial(_attn_sample_kernel, n_pages_step=pps, n_heads=n_heads, hd=hd, scale=hd ** -0.5,
                          lam_init=_lambda_init(layer)),
        grid_spec=grid_spec, out_shape=jax.ShapeDtypeStruct((ts, n_heads * hw), BF16),
        compiler_params=_cparams(2), name="attn_sample")(*ins)


def _shift_rows(x, k, t_idx):
    return jnp.where(t_idx >= k, pltpu.roll(x, k, 0), 0.0)


def _pool_prompt_kernel(u_ref, w_ref, sc_ref, o_ref, *, gw):
    t = u_ref.shape[0]
    t_idx = lax.broadcasted_iota(jnp.int32, (t, 1), 0)
    for g, win in enumerate(POOL_WINDOWS):
        cols = slice(g * gw, (g + 1) * gw)
        u = u_ref[:, cols]
        s = u
        k = 1
        while k < win:
            s = s + _shift_rows(s, k, t_idx)
            k *= 2
        cnt = jnp.minimum(t_idx + 1, win).astype(F32)
        pooled = (s / cnt - u).astype(BF16)
        mixed = jnp.dot(pooled, w_ref[g].astype(BF16), preferred_element_type=F32)
        o_ref[:, cols] = (mixed * sc_ref[:, cols]).astype(o_ref.dtype)


def _pool_prompt(z, pool_w, pool_scale, layer, *, bp, t, off_pool):
    n_g, gw = pool_w.shape[1], pool_w.shape[2]
    pw = n_g * gw
    return pl.pallas_call(
        functools.partial(_pool_prompt_kernel, gw=gw),
        grid=(bp,),
        in_specs=[pl.BlockSpec((t, pw), lambda b: (b, off_pool // pw)),
                  pl.BlockSpec((None, n_g, gw, gw), lambda b: (layer, 0, 0, 0)),
                  pl.BlockSpec((None, 1, pw), lambda b: (layer, 0, 0))],
        out_specs=pl.BlockSpec((t, pw), lambda b: (b, 0)),
        out_shape=jax.ShapeDtypeStruct((bp * t, pw), BF16),
        compiler_params=_cparams(1), name="pool_prompt")(z, pool_w, pool_scale)


def _conv_prompt_kernel(cb_ref, cc_ref, cx_ref, w_ref, o_ref, tail_ref):
    t = cb_ref.shape[0]
    t_idx = lax.broadcasted_iota(jnp.int32, (t, 1), 0)
    u = cc_ref[...] * cx_ref[...]
    w = w_ref[...]
    zc = w[0:1] * _shift_rows(u, 2, t_idx) + w[1:2] * _shift_rows(u, 1, t_idx) + w[2:3] * u
    o_ref[...] = (cb_ref[...] * zc).astype(o_ref.dtype)
    tail_ref[...] = u[t - 8:, :]


def _conv_prompt(z, conv_w, layer, *, bp, t, off_cb, off_cc, off_cx):
    cw = conv_w.shape[2]
    cbk = _pick_tile(cw, 256, V7X_LANES)
    ncb = cw // cbk

    def zspec(off):
        return pl.BlockSpec((t, cbk), lambda b, c: (b, off // cbk + c))

    return pl.pallas_call(
        _conv_prompt_kernel, grid=(bp, ncb),
        in_specs=[zspec(off_cb), zspec(off_cc), zspec(off_cx),
                  pl.BlockSpec((None, conv_w.shape[1], cbk), lambda b, c: (layer, 0, c))],
        out_specs=[pl.BlockSpec((t, cbk), lambda b, c: (b, c)),
                   pl.BlockSpec((None, 8, cbk), lambda b, c: (b, 0, c))],
        out_shape=[jax.ShapeDtypeStruct((bp * t, cw), BF16), jax.ShapeDtypeStruct((bp, 8, cw), F32)],
        compiler_params=_cparams(2), name="conv_prompt")(z, z, z, conv_w)


def _mix_sample_kernel(pu_ref, cb_ref, cc_ref, cx_ref, hp_ref, hc_ref, pw_ref, ps_ref, cw_ref,
                       pm_ref, cm_ref, cu_ref, *, bs, gw, past_len):
    n_hist = hp_ref.shape[0]
    pm_ref[...] = jnp.zeros(pm_ref.shape, pm_ref.dtype)
    cm_ref[...] = jnp.zeros(cm_ref.shape, cm_ref.dtype)
    for g, win in enumerate(POOL_WINDOWS):
        cols = slice(g * gw, (g + 1) * gw)
        u = pu_ref[:, cols]
        s = u
        for i in range(1, win):
            s = s + hp_ref[n_hist - i][:, cols]
        cnt = float(min(past_len + 1, win))
        pooled = (s / cnt - u).astype(BF16)
        mixed = jnp.dot(pooled, pw_ref[g].astype(BF16), preferred_element_type=F32)
        pm_ref[0:bs, cols] = (mixed * ps_ref[:, cols]).astype(pm_ref.dtype)
    u = cc_ref[...] * cx_ref[...]
    w = cw_ref[...]
    zc = w[0:1] * hc_ref[0] + w[1:2] * hc_ref[1] + w[2:3] * u
    cm_ref[0:bs, :] = (cb_ref[...] * zc).astype(cm_ref.dtype)
    cu_ref[...] = u


def _mix_sample(z, hist_pool, hist_conv, pool_w, pool_scale, conv_w, layer, *,
                row0, ts, bs, off_pool, off_cb, off_cc, off_cx, past_len):
    n_g, gw = pool_w.shape[1], pool_w.shape[2]
    pw = n_g * gw
    cw = conv_w.shape[2]

    def zspec(off, width):
        return pl.BlockSpec((bs, width), lambda i: (row0 // bs, off // width))

    return pl.pallas_call(
        functools.partial(_mix_sample_kernel, bs=bs, gw=gw, past_len=past_len),
        grid=(1,),
        in_specs=[zspec(off_pool, pw), zspec(off_cb, cw), zspec(off_cc, cw), zspec(off_cx, cw),
                  pl.BlockSpec(hist_pool.shape, lambda i: (0, 0, 0)),
                  pl.BlockSpec(hist_conv.shape, lambda i: (0, 0, 0)),
                  pl.BlockSpec((None, n_g, gw, gw), lambda i: (layer, 0, 0, 0)),
                  pl.BlockSpec((None, 1, pw), lambda i: (layer, 0, 0)),
                  pl.BlockSpec((None, conv_w.shape[1], cw), lambda i: (layer, 0, 0))],
        out_specs=[pl.BlockSpec((ts, pw), lambda i: (0, 0)),
                   pl.BlockSpec((ts, cw), lambda i: (0, 0)),
                   pl.BlockSpec((bs, cw), lambda i: (0, 0))],
        out_shape=[jax.ShapeDtypeStruct((ts, pw), BF16), jax.ShapeDtypeStruct((ts, cw), BF16),
                   jax.ShapeDtypeStruct((bs, cw), F32)],
        compiler_params=_cparams(1), name="mix_sample")(
            z, z, z, z, hist_pool, hist_conv, pool_w, pool_scale, conv_w)


def _merge_kernel(pm_ref, oa_ref, cm_ref, g0_ref, g1_ref, g2_ref, wp_ref, wa_ref, wc_ref, o_ref,
                  wp_sc, wa_sc, wc_sc):
    @pl.when(pl.program_id(1) == 0)
    def _():
        wp_sc[...] = wp_ref[...].astype(BF16)
        wa_sc[...] = wa_ref[...].astype(BF16)
        wc_sc[...] = wc_ref[...].astype(BF16)

    br_pool = jnp.dot(pm_ref[...], wp_sc[...], preferred_element_type=F32)
    br_attn = jnp.dot(oa_ref[...], wa_sc[...], preferred_element_type=F32)
    br_conv = jnp.dot(cm_ref[...], wc_sc[...], preferred_element_type=F32)
    merged = (jax.nn.sigmoid(g0_ref[...]) * br_pool + jax.nn.sigmoid(g1_ref[...]) * br_attn
              + jax.nn.sigmoid(g2_ref[...]) * br_conv)
    o_ref[...] = merged.astype(o_ref.dtype)


def _merge(pm, oa, cm, z, w_up_pool, w_up_attn, w_up_conv, layer, *, off_g, d, tm, tn):
    m = pm.shape[0]
    pw, aw, cw = pm.shape[1], oa.shape[1], cm.shape[1]

    def act(width):
        return pl.BlockSpec((tm, width), lambda j, i: (i, 0))

    def gate(k):
        return pl.BlockSpec((tm, tn), lambda j, i: (i, (off_g + k * d) // tn + j))

    def wspec(width):
        return pl.BlockSpec((None, width, tn), lambda j, i: (layer, 0, j))

    return pl.pallas_call(
        _merge_kernel, grid=(d // tn, m // tm),
        in_specs=[act(pw), act(aw), act(cw), gate(0), gate(1), gate(2), wspec(pw), wspec(aw), wspec(cw)],
        out_specs=pl.BlockSpec((tm, tn), lambda j, i: (i, j)),
        out_shape=jax.ShapeDtypeStruct((m, d), BF16),
        scratch_shapes=[pltpu.VMEM((pw, tn), BF16), pltpu.VMEM((aw, tn), BF16), pltpu.VMEM((cw, tn), BF16)],
        compiler_params=_cparams(2), name="merge")(pm, oa, cm, z, z, z, w_up_pool, w_up_attn, w_up_conv)


def _swiglu_step(h, w1_ref, w3_ref, w2_ref):
    a = jnp.dot(h, w1_ref[...].astype(BF16), preferred_element_type=F32)
    b = jnp.dot(h, w3_ref[...].astype(BF16), preferred_element_type=F32)
    act = (a * jax.nn.sigmoid(a)) * b
    return jnp.dot(act.astype(BF16), w2_ref[...].astype(BF16), preferred_element_type=F32)


def _ffn_kernel(h_ref, w1_ref, w3_ref, w2_ref, o_ref):
    @pl.when(pl.program_id(1) == 0)
    def _():
        o_ref[...] = jnp.zeros(o_ref.shape, o_ref.dtype)

    o_ref[...] += _swiglu_step(h_ref[...], w1_ref, w3_ref, w2_ref)


def _ffn(h, w1, w3, w2, idx, *, tm, tf):
    m, d = h.shape
    d_ff = w1.shape[2]
    up = pl.BlockSpec((None, d, tf), lambda i, f: (idx, 0, f))
    down = pl.BlockSpec((None, tf, d), lambda i, f: (idx, f, 0))
    return pl.pallas_call(
        _ffn_kernel, grid=(m // tm, d_ff // tf),
        in_specs=[pl.BlockSpec((tm, d), lambda i, f: (i, 0)), up, up, down],
        out_specs=pl.BlockSpec((tm, d), lambda i, f: (i, 0)),
        out_shape=jax.ShapeDtypeStruct((m, d), F32),
        compiler_params=_cparams(2), name="ffn")(h, w1, w3, w2)


def _cumsum_rows(counts):
    n, e = counts.shape
    blk = _pick_tile(n, V7X_LANES, 1)
    c = counts.reshape(n // blk, blk, e)
    tri = (jnp.arange(blk)[:, None] >= jnp.arange(blk)[None, :]).astype(F32)
    within = jnp.einsum("ij,bje->bie", tri, c.astype(F32)).astype(jnp.int32)
    before = jnp.cumsum(within[:, -1, :], axis=0) - within[:, -1, :]
    return (within + before[:, None, :]).reshape(n, e)


def _route_tables(route, n_valid, n_experts, tile_rows):
    m = route.shape[0]
    idx = route[:, :TOP_K].astype(jnp.int32)
    valid = jnp.arange(m) < n_valid
    wts = jnp.where(valid[:, None], route[:, TOP_K:2 * TOP_K], 0.0)
    onehot = ((idx[:, :, None] == jnp.arange(n_experts)) & valid[:, None, None]).astype(jnp.int32)
    onehot = onehot.reshape(m * TOP_K, n_experts)
    csum = _cumsum_rows(onehot)
    rank = jnp.sum((csum - onehot) * onehot, axis=1)
    tiles_e = (csum[-1] + tile_rows - 1) // tile_rows
    tile_end = jnp.cumsum(tiles_e)
    start_row = (tile_end - tiles_e) * tile_rows
    valid2 = jnp.repeat(valid, TOP_K)
    pos = jnp.where(valid2, start_row[idx.reshape(-1)] + rank, 0)
    n_tiles = (n_valid * TOP_K) // tile_rows + n_experts
    n_rows = n_tiles * tile_rows
    tok = jnp.repeat(jnp.arange(m, dtype=jnp.int32), TOP_K)
    src = jnp.zeros((n_rows,), jnp.int32).at[jnp.where(valid2, pos, n_rows)].set(tok, mode="drop")
    n_used = tile_end[-1]
    tile_id = jnp.minimum(jnp.arange(n_tiles), n_used - 1)
    tile_expert = jnp.sum((tile_end[None, :] <= tile_id[:, None]).astype(jnp.int32), axis=1)
    tile_expert = jnp.clip(tile_expert, 0, n_experts - 1).astype(jnp.int32)
    return (src.reshape(n_tiles, 1, tile_rows), tile_expert, n_used.reshape(1).astype(jnp.int32),
            pos.astype(jnp.int32), wts)


def _row_copy(src_hbm, src_row, dst_vmem, dst_row, sem):
    return pltpu.make_async_copy(src_hbm.at[pl.ds(src_row, 1), :], dst_vmem.at[pl.ds(dst_row, 1), :], sem)


def _moe_group_kernel(te_ref, nu_ref, src_ref, h_hbm, w1_ref, w3_ref, w2_ref, o_ref, x32_sc, x16_sc, sem):
    del te_ref
    i = pl.program_id(0)
    f = pl.program_id(1)
    used = i < nu_ref[0]
    rows = x32_sc.shape[0]

    @pl.when(f == 0)
    def _():
        o_ref[...] = jnp.zeros(o_ref.shape, o_ref.dtype)

    @pl.when(used & (f == 0))
    def _():
        def issue(r, carry):
            _row_copy(h_hbm, src_ref[0, r], x32_sc, r, sem).start()
            return carry

        def wait(r, carry):
            _row_copy(h_hbm, 0, x32_sc, r, sem).wait()
            return carry

        lax.fori_loop(0, rows, issue, 0)
        lax.fori_loop(0, rows, wait, 0)
        x16_sc[...] = x32_sc[...].astype(BF16)

    @pl.when(used)
    def _():
        o_ref[...] += _swiglu_step(x16_sc[...], w1_ref, w3_ref, w2_ref)


def _moe_group(h32, src, tile_expert, n_used, w1, w3, w2, idx, *, tf):
    n_tiles, _, rows = src.shape
    d = h32.shape[1]
    d_ff = w1.shape[3]
    nf = d_ff // tf

    def f_blk(i, f, nu):
        return jnp.where(i < nu[0], f, nf - 1)

    up = pl.BlockSpec((None, None, d, tf), lambda i, f, te, nu: (idx, te[i], 0, f_blk(i, f, nu)))
    down = pl.BlockSpec((None, None, tf, d), lambda i, f, te, nu: (idx, te[i], f_blk(i, f, nu), 0))
    grid_spec = pltpu.PrefetchScalarGridSpec(
        num_scalar_prefetch=2, grid=(n_tiles, nf),
        in_specs=[pl.BlockSpec((None, 1, rows), lambda i, f, te, nu: (i, 0, 0), memory_space=pltpu.SMEM),
                  pl.BlockSpec(memory_space=pl.ANY), up, up, down],
        out_specs=pl.BlockSpec((rows, d), lambda i, f, te, nu: (i, 0)),
        scratch_shapes=[pltpu.VMEM((rows, d), F32), pltpu.VMEM((rows, d), BF16), pltpu.SemaphoreType.DMA(())])
    return pl.pallas_call(
        _moe_group_kernel, grid_spec=grid_spec,
        out_shape=jax.ShapeDtypeStruct((n_tiles * rows, d), F32),
        compiler_params=_cparams(2), name="moe_group")(tile_expert, n_used, src, h32, w1, w3, w2)


def _moe_combine_kernel(pos_ref, w_ref, y_hbm, o_ref, y_sc, sem):
    rows = o_ref.shape[0]

    def issue(r, carry):
        for s in range(TOP_K):
            _row_copy(y_hbm, pos_ref[0, TOP_K * r + s], y_sc.at[s], r, sem).start()
        return carry

    def wait(r, carry):
        for s in range(TOP_K):
            _row_copy(y_hbm, 0, y_sc.at[s], r, sem).wait()
        return carry

    lax.fori_loop(0, rows, issue, 0)
    lax.fori_loop(0, rows, wait, 0)
    w = w_ref[...]
    acc = w[:, 0:1] * y_sc[0]
    for s in range(1, TOP_K):
        acc = acc + w[:, s:s + 1] * y_sc[s]
    o_ref[...] = acc


def _moe_combine(y, pos, wts, *, tm):
    m = wts.shape[0]
    d = y.shape[1]
    return pl.pallas_call(
        _moe_combine_kernel, grid=(m // tm,),
        in_specs=[pl.BlockSpec((None, 1, TOP_K * tm), lambda i: (i, 0, 0), memory_space=pltpu.SMEM),
                  pl.BlockSpec((tm, TOP_K), lambda i: (i, 0)),
                  pl.BlockSpec(memory_space=pl.ANY)],
        out_specs=pl.BlockSpec((tm, d), lambda i: (i, 0)),
        out_shape=jax.ShapeDtypeStruct((m, d), F32),
        scratch_shapes=[pltpu.VMEM((TOP_K, tm, d), F32), pltpu.SemaphoreType.DMA(())],
        compiler_params=_cparams(1), name="moe_combine")(pos.reshape(m // tm, 1, TOP_K * tm), wts, y)


def _moe(h32, route, w1, w3, w2, idx, *, n_valid, tile_rows, tf, tm_combine):
    src, tile_expert, n_used, pos, wts = _route_tables(route, n_valid, w1.shape[1], tile_rows)
    y = _moe_group(h32, src, tile_expert, n_used, w1, w3, w2, idx, tf=tf)
    return _moe_combine(y, pos, wts, tm=tm_combine)


def kernel(x_prompt, x_sample, cache_k, cache_v, state_pool, state_conv, page_table, meta_tokens, norm_mix, w_in, lambda_q1, lambda_k1, lambda_q2, lambda_k2, subln_g, pool_w, pool_scale, conv_w, w_up_pool, w_up_attn, w_up_conv, w_o, norm_ffn, ffn_w1, ffn_w3, ffn_w2, router_w, moe_w1, moe_w3, moe_w2, norm_final):
    bp, seq, d = x_prompt.shape
    bs, dec_seq, _ = x_sample.shape
    assert dec_seq == 1 and bs % 8 == 0
    depth = w_in.shape[0]
    n_meta = meta_tokens.shape[0]
    t = seq + n_meta
    hd = cache_k.shape[-1]
    n_heads = cache_v.shape[-2]
    hw = 2 * hd
    att_w = n_heads * hw
    pw = state_pool.shape[-1]
    cw = state_conv.shape[-1]
    page = cache_k.shape[2]
    past_len = page_table.shape[1] * page
    off_q, off_k, off_v = 0, att_w, 2 * att_w
    off_pool = 3 * att_w
    off_cb = off_pool + pw
    off_cc, off_cx, off_g = off_cb + cw, off_cb + 2 * cw, off_cb + 3 * cw
    in_w = off_g + 3 * d
    assert w_in.shape[2] == in_w and state_pool.shape[2] == max(POOL_WINDOWS) - 1

    n_prompt = bp * t
    ts = next(c for c in (64, 32, 16, 8) if n_prompt % c == 0 and c >= bs)
    m_pad = n_prompt + ts
    tm = _pick_tile(m_pad, ROW_TILE_TARGET)
    tq = _pick_tile(t, SEQ_TILE_TARGET)
    tn_in = _pick_tile(in_w, 1024, V7X_LANES)
    tn_d = _pick_tile(d, 512, V7X_LANES)
    tf = _pick_tile(ffn_w1.shape[2], 256, V7X_LANES)
    tm_merge = _pick_tile(m_pad, ROW_TILE_TARGET // 2)
    tm_norm = _pick_tile(m_pad, NORM_TILE_TARGET)
    assert n_prompt % bs == 0 and all(o % w == 0 for o, w in
                                      ((off_k, hw), (off_v, hw), (off_pool, pw), (off_cb, cw), (off_g, tn_d)))

    dt = x_prompt.dtype
    meta = jnp.broadcast_to(meta_tokens.astype(dt)[None], (bp, n_meta, d))
    x = jnp.concatenate([jnp.concatenate([meta, x_prompt], axis=1).reshape(n_prompt, d),
                         x_sample.reshape(bs, d), jnp.zeros((ts - bs, d), dt)], axis=0)

    lam_vecs = [v.reshape(depth, 1, hd) for v in (lambda_q1, lambda_k1, lambda_q2, lambda_k2)]
    subln = subln_g.reshape(depth, 1, hw)
    pool_sc = pool_scale.reshape(depth, 1, pw)

    kp_l, vp_l, pp_l, cp_l, ks_l, vs_l, ps_l, cs_l = ([] for _ in range(8))
    f = None
    for l in range(depth):
        x, h, _ = _norm(x, f, norm_mix[l], None, tm=tm_norm, out_dtype=BF16)
        z = _matmul(h, w_in, l, None, tm=tm, tn=tn_in, out_dtype=F32)

        zs = z[n_prompt:n_prompt + bs]
        q_s = zs[:, off_q:off_k].reshape(bs, 2 * n_heads, hd)
        k_s = zs[:, off_k:off_v].reshape(bs, 2 * n_heads, hd)
        v_s = zs[:, off_v:off_pool].reshape(bs, n_heads, hw)

        def prompt_cols(c0, c1):
            return lax.slice(z, (0, c0), (n_prompt, c1)).reshape(bp, t, c1 - c0)

        oa_p = _attn_prompt(z, lam_vecs, subln, l, bp=bp, t=t, n_heads=n_heads, hd=hd,
                            off_q=off_q, off_k=off_k, off_v=off_v, tq=tq)
        oa_s = _attn_sample(q_s, k_s, v_s, cache_k, cache_v, page_table, lam_vecs, subln, l,
                            ts=ts, n_heads=n_heads, hd=hd)
        pm_p = _pool_prompt(z, pool_w, pool_sc, l, bp=bp, t=t, off_pool=off_pool)
        cm_p, conv_tail = _conv_prompt(z, conv_w, l, bp=bp, t=t, off_cb=off_cb, off_cc=off_cc, off_cx=off_cx)
        pm_s, cm_s, cu_s = _mix_sample(z, jnp.swapaxes(state_pool[l], 0, 1), jnp.swapaxes(state_conv[l], 0, 1),
                                       pool_w, pool_sc, conv_w, l, row0=n_prompt, ts=ts, bs=bs, off_pool=off_pool,
                                       off_cb=off_cb, off_cc=off_cc, off_cx=off_cx, past_len=past_len)
        oa = jnp.concatenate([oa_p, oa_s], axis=0)
        pm = jnp.concatenate([pm_p, pm_s], axis=0)
        cm = jnp.concatenate([cm_p, cm_s], axis=0)
        merged = _merge(pm, oa, cm, z, w_up_pool, w_up_attn, w_up_conv, l, off_g=off_g, d=d,
                        tm=tm_merge, tn=tn_d)
        x = _matmul(merged, w_o, l, x, tm=tm, tn=tn_d, out_dtype=F32)

        if l % 2 == 0:
            _, h2, _ = _norm(x, None, norm_ffn[l], None, tm=tm_norm, out_dtype=BF16)
            f = _ffn(h2, ffn_w1, ffn_w3, ffn_w2, l // 2, tm=tm, tf=tf)
        else:
            _, h2, route = _norm(x, None, norm_ffn[l], router_w[l // 2], tm=tm_norm, out_dtype=F32)
            f = _moe(h2, route, moe_w1, moe_w3, moe_w2, l // 2, n_valid=n_prompt + bs, tile_rows=tm, tf=tf,
                     tm_combine=tm_merge)

        kp_l.append(prompt_cols(off_k, off_v).reshape(bp, t, 2 * n_heads, hd))
        vp_l.append(prompt_cols(off_v, off_pool).reshape(bp, t, n_heads, hw))
        pp_l.append(prompt_cols(off_pool, off_cb)[:, t - state_pool.shape[2]:])
        cp_l.append(conv_tail[:, 8 - state_conv.shape[2]:])
        ks_l.append(k_s.reshape(bs, 1, 2 * n_heads, hd))
        vs_l.append(v_s.reshape(bs, 1, n_heads, hw))
        ps_l.append(jnp.concatenate([state_pool[l][:, 1:], zs[:, None, off_pool:off_cb]], axis=1))
        cs_l.append(jnp.concatenate([state_conv[l][:, 1:], cu_s[:, None]], axis=1))

    _, y, _ = _norm(x, f, norm_final, None, tm=tm_norm, out_dtype=F32)
    y_prompt = y[:n_prompt].reshape(bp, t, d)[:, n_meta:]
    y_sample = y[n_prompt:n_prompt + bs].reshape(bs, 1, d)
    return (y_prompt, y_sample,
            jnp.stack(kp_l, axis=1), jnp.stack(vp_l, axis=1), jnp.stack(pp_l, axis=0), jnp.stack(cp_l, axis=0),
            jnp.stack(ks_l, axis=1), jnp.stack(vs_l, axis=1), jnp.stack(ps_l, axis=0), jnp.stack(cs_l, axis=0))
```

```python
import functools
import math

import jax
import jax.numpy as jnp
from jax import lax
from jax.experimental import pallas as pl
from jax.experimental.pallas import tpu as pltpu

F32 = jnp.float32
BF16 = jnp.bfloat16

POOL_WINDOWS = (2, 4, 8, 16)
TOP_K = 2
RMS_EPS = 1e-6
SUBLN_EPS = 1e-5
NEG_INF = -1e30

V7X_LANES = 128
V7X_VMEM_LIMIT_BYTES = 56 * 1024 * 1024

ROW_TILE_TARGET = 1040
NORM_TILE_TARGET = 416
SEQ_TILE_TARGET = 344
SAMPLE_PAGES_PER_STEP = 4


def _lambda_init(l):
    return 0.8 - 0.6 * math.exp(-0.3 * l)


def _pick_tile(n, target, mult=8):
    best = None
    for t in range(mult, min(n, target) + 1, mult):
        if n % t == 0:
            best = t
    assert best is not None, (n, target, mult)
    return best


def _cparams(n_axes):
    return pltpu.CompilerParams(dimension_semantics=("arbitrary",) * n_axes,
                                vmem_limit_bytes=V7X_VMEM_LIMIT_BYTES)


def _norm_kernel(*refs, has_add, n_experts, eps):
    refs = list(refs)
    x_ref = refs.pop(0)
    f_ref = refs.pop(0) if has_add else None
    g_ref = refs.pop(0)
    r_ref = refs.pop(0) if n_experts else None
    xo_ref = refs.pop(0) if has_add else None
    h_ref = refs.pop(0)
    gate_ref = refs.pop(0) if n_experts else None

    x = x_ref[...]
    if has_add:
        x = x + f_ref[...]
        xo_ref[...] = x
    ms = jnp.mean(x * x, axis=-1, keepdims=True)
    y = x * lax.rsqrt(ms + eps) * g_ref[...]
    h_ref[...] = y.astype(h_ref.dtype)
    if n_experts:
        logits = jnp.dot(y.astype(BF16), r_ref[...].astype(BF16), preferred_element_type=F32)
        lane = lax.broadcasted_iota(jnp.int32, logits.shape, 1)
        logits = jnp.where(lane < n_experts, logits, NEG_INF)
        v1 = jnp.max(logits, axis=-1, keepdims=True)
        i1 = jnp.min(jnp.where(logits == v1, lane, V7X_LANES), axis=-1, keepdims=True)
        rest = jnp.where(lane == i1, NEG_INF, logits)
        v2 = jnp.max(rest, axis=-1, keepdims=True)
        i2 = jnp.min(jnp.where(rest == v2, lane, V7X_LANES), axis=-1, keepdims=True)
        e2 = jnp.exp(v2 - v1)
        w1 = 1.0 / (1.0 + e2)
        w2 = e2 / (1.0 + e2)
        gate_ref[...] = jnp.where(lane == 0, i1.astype(F32), jnp.where(lane == 1, i2.astype(F32),
                                  jnp.where(lane == 2, w1, jnp.where(lane == 3, w2, 0.0))))


def _norm(x, f, g, router, *, tm, out_dtype, eps=RMS_EPS):
    m, d = x.shape
    has_add = f is not None
    n_experts = 0 if router is None else router.shape[1]
    row = pl.BlockSpec((tm, d), lambda i: (i, 0))
    ins, in_specs = [x], [row]
    if has_add:
        ins.append(f)
        in_specs.append(row)
    ins.append(g.reshape(1, d))
    in_specs.append(pl.BlockSpec((1, d), lambda i: (0, 0)))
    if n_experts:
        ins.append(jnp.pad(router, ((0, 0), (0, V7X_LANES - n_experts))))
        in_specs.append(pl.BlockSpec((d, V7X_LANES), lambda i: (0, 0)))
    out_shape, out_specs = [], []
    if has_add:
        out_shape.append(jax.ShapeDtypeStruct((m, d), F32))
        out_specs.append(row)
    out_shape.append(jax.ShapeDtypeStruct((m, d), out_dtype))
    out_specs.append(row)
    if n_experts:
        out_shape.append(jax.ShapeDtypeStruct((m, V7X_LANES), F32))
        out_specs.append(pl.BlockSpec((tm, V7X_LANES), lambda i: (i, 0)))
    outs = pl.pallas_call(
        functools.partial(_norm_kernel, has_add=has_add, n_experts=n_experts, eps=eps),
        grid=(m // tm,), in_specs=in_specs, out_specs=out_specs, out_shape=out_shape,
        compiler_params=_cparams(1), name="rmsnorm")(*ins)
    outs = list(outs)
    x_new = outs.pop(0) if has_add else x
    h = outs.pop(0)
    gate = outs.pop(0) if n_experts else None
    return x_new, h, gate


def _mm_kernel(*refs, has_res):
    if has_res:
        x_ref, w_ref, r_ref, o_ref, w_sc = refs
    else:
        x_ref, w_ref, o_ref, w_sc = refs

    @pl.when(pl.program_id(1) == 0)
    def _():
        w_sc[...] = w_ref[...].astype(BF16)

    acc = jnp.dot(x_ref[...], w_sc[...], preferred_element_type=F32)
    if has_res:
        acc = r_ref[...] + acc
    o_ref[...] = acc.astype(o_ref.dtype)


def _matmul(x, w_stack, layer, res, *, tm, tn, out_dtype):
    m, k = x.shape
    n = w_stack.shape[2]
    ins = [x, w_stack]
    in_specs = [pl.BlockSpec((tm, k), lambda j, i: (i, 0)),
                pl.BlockSpec((None, k, tn), lambda j, i: (layer, 0, j))]
    if res is not None:
        ins.append(res)
        in_specs.append(pl.BlockSpec((tm, tn), lambda j, i: (i, j)))
    return pl.pallas_call(
        functools.partial(_mm_kernel, has_res=res is not None),
        grid=(n // tn, m // tm), in_specs=in_specs,
        out_specs=pl.BlockSpec((tm, tn), lambda j, i: (i, j)),
        out_shape=jax.ShapeDtypeStruct((m, n), out_dtype),
        scratch_shapes=[pltpu.VMEM((k, tn), BF16)],
        compiler_params=_cparams(2), name="matmul")(*ins)


def _diff_lambda(lq1_ref, lk1_ref, lq2_ref, lk2_ref, lam_init):
    a = jnp.sum(lq1_ref[...] * lk1_ref[...], axis=-1, keepdims=True)
    b = jnp.sum(lq2_ref[...] * lk2_ref[...], axis=-1, keepdims=True)
    return jnp.exp(a) - jnp.exp(b) + lam_init


def _subln(o, g, lam_init):
    ms = jnp.mean(o * o, axis=-1, keepdims=True)
    return o * lax.rsqrt(ms + SUBLN_EPS) * g * (1.0 - lam_init)


def _attn_prompt_kernel(q_ref, k_ref, v_ref, lq1_ref, lk1_ref, lq2_ref, lk2_ref, g_ref, o_ref,
                        s_sc, red_sc, acc_sc, *, tq, hd, scale, lam_init):
    qi = pl.program_id(2)
    q = q_ref[...]
    qs = [q[:, :hd].astype(BF16), q[:, hd:].astype(BF16)]
    row = lax.broadcasted_iota(jnp.int32, (tq, tq), 0)
    col = lax.broadcasted_iota(jnp.int32, (tq, tq), 1)

    def scores(c, i):
        start = pl.multiple_of(c * tq, 8)
        k = k_ref[pl.ds(start, tq), i * hd:(i + 1) * hd].astype(BF16)
        return lax.dot_general(qs[i], k, (((1,), (1,)), ((), ())), preferred_element_type=F32) * scale

    for i in range(2):
        s = jnp.where(col <= row, scores(qi, i), NEG_INF)
        s_sc[i, qi] = s
        red_sc[i] = s

    def pass1(c, carry):
        for i in range(2):
            s = scores(c, i)
            s_sc[i, c] = s
            red_sc[i] = jnp.maximum(red_sc[i], s)
        return carry

    lax.fori_loop(0, qi, pass1, 0)

    for i in range(2):
        m = jnp.max(red_sc[i], axis=-1, keepdims=True)
        s_sc[i, qi] = s_sc[i, qi] - m
        red_sc[2 + i] = jnp.broadcast_to(m, (tq, tq))

    def probs(c, i, on_diagonal):
        s = s_sc[i, c]
        return jnp.exp(s if on_diagonal else s - red_sc[2 + i])

    v_d = v_ref[pl.ds(pl.multiple_of(qi * tq, 8), tq), :].astype(BF16)
    for i in range(2):
        p = probs(qi, i, True)
        red_sc[i] = p
        acc_sc[i] = jnp.dot(p.astype(BF16), v_d, preferred_element_type=F32)

    def pass2(c, carry):
        v = v_ref[pl.ds(pl.multiple_of(c * tq, 8), tq), :].astype(BF16)
        for i in range(2):
            p = probs(c, i, False)
            red_sc[i] = red_sc[i] + p
            acc_sc[i] = acc_sc[i] + jnp.dot(p.astype(BF16), v, preferred_element_type=F32)
        return carry

    lax.fori_loop(0, qi, pass2, 0)

    lam = _diff_lambda(lq1_ref, lk1_ref, lq2_ref, lk2_ref, lam_init)
    outs = [acc_sc[i] / jnp.sum(red_sc[i], axis=-1, keepdims=True) for i in range(2)]
    o_ref[...] = _subln(outs[0] - lam * outs[1], g_ref[...], lam_init).astype(o_ref.dtype)


def _lambda_specs(n_grid_axes, layer, hd):
    if n_grid_axes == 3:
        return [pl.BlockSpec((None, 1, hd), lambda b, h, i: (layer, 0, 0))] * 4
    return [pl.BlockSpec((None, 1, hd), lambda b, j, pt: (layer, 0, 0))] * 4


def _attn_prompt(z, lams, subln_g, layer, *, bp, t, n_heads, hd, off_q, off_k, off_v, tq):
    nq = t // tq
    hw = 2 * hd
    return pl.pallas_call(
        functools.partial(_attn_prompt_kernel, tq=tq, hd=hd, scale=hd ** -0.5, lam_init=_lambda_init(layer)),
        grid=(bp, n_heads, nq),
        in_specs=[pl.BlockSpec((tq, hw), lambda b, h, i: (b * nq + i, off_q // hw + h)),
                  pl.BlockSpec((t, hw), lambda b, h, i: (b, off_k // hw + h)),
                  pl.BlockSpec((t, hw), lambda b, h, i: (b, off_v // hw + h)),
                  *_lambda_specs(3, layer, hd),
                  pl.BlockSpec((None, 1, hw), lambda b, h, i: (layer, 0, 0))],
        out_specs=pl.BlockSpec((tq, hw), lambda b, h, i: (b * nq + i, h)),
        out_shape=jax.ShapeDtypeStruct((bp * t, n_heads * hw), BF16),
        scratch_shapes=[pltpu.VMEM((2, nq, tq, tq), F32), pltpu.VMEM((4, tq, tq), F32), pltpu.VMEM((2, tq, hw), F32)],
        compiler_params=_cparams(3), name="attn_prompt")(z, z, z, *lams, subln_g)


def _attn_sample_kernel(pt_ref, q_ref, kn_ref, vn_ref, lq1_ref, lk1_ref, lq2_ref, lk2_ref, g_ref, *rest,
                        n_pages_step, n_heads, hd, scale, lam_init):
    k_refs = rest[:n_pages_step]
    v_refs = rest[n_pages_step:2 * n_pages_step]
    o_ref, m_sc, l_sc, acc_sc, rows_sc = rest[2 * n_pages_step:]
    b = pl.program_id(0)
    j = pl.program_id(1)
    last_b = pl.num_programs(0) - 1
    last_j = pl.num_programs(1) - 1
    page = k_refs[0].shape[0]
    hw = 2 * hd

    @pl.when((b == 0) & (j == 0))
    def _():
        rows_sc[...] = jnp.zeros(rows_sc.shape, F32)

    @pl.when(j == 0)
    def _():
        m_sc[...] = jnp.full(m_sc.shape, NEG_INF, F32)
        l_sc[...] = jnp.zeros(l_sc.shape, F32)
        acc_sc[...] = jnp.zeros(acc_sc.shape, F32)

    qs = [q_ref[pl.ds(c, n_heads, stride=2), :].astype(BF16) for c in range(2)]
    lane_head = lax.broadcasted_iota(jnp.int32, (n_heads, page * n_heads), 1) % n_heads
    own_head = lane_head == lax.broadcasted_iota(jnp.int32, (n_heads, page * n_heads), 0)

    scores = []
    for i in range(n_pages_step):
        per_map = []
        for c in range(2):
            kc = k_refs[i][:, pl.ds(c, n_heads, stride=2), :].reshape(page * n_heads, hd).astype(BF16)
            s = lax.dot_general(qs[c], kc, (((1,), (1,)), ((), ())), preferred_element_type=F32) * scale
            per_map.append(jnp.where(own_head, s, NEG_INF))
        scores.append(per_map)

    m_new, alpha = [], []
    for c in range(2):
        m_cur = m_sc[c]
        for i in range(n_pages_step):
            m_cur = jnp.maximum(m_cur, jnp.max(scores[i][c], axis=-1, keepdims=True))
        m_new.append(m_cur)
        alpha.append(jnp.exp(m_sc[c] - m_cur))
        m_sc[c] = m_cur

    l_add = [jnp.zeros((n_heads, 1), F32)] * 2
    acc_add = jnp.zeros((2 * n_heads, hw), F32)
    for i in range(n_pages_step):
        p = [jnp.exp(scores[i][c] - m_new[c]) for c in range(2)]
        l_add = [l_add[c] + jnp.sum(p[c], axis=-1, keepdims=True) for c in range(2)]
        v = v_refs[i][...].reshape(page * n_heads, hw).astype(BF16)
        acc_add = acc_add + jnp.dot(jnp.concatenate(p, axis=0).astype(BF16), v, preferred_element_type=F32)
    for c in range(2):
        l_sc[c] = alpha[c] * l_sc[c] + l_add[c]
        acc_sc[c] = alpha[c] * acc_sc[c] + acc_add[c * n_heads:(c + 1) * n_heads]

    @pl.when(j == last_j)
    def _():
        vn = vn_ref[...].astype(BF16).astype(F32)
        outs = []
        for c in range(2):
            kn = kn_ref[pl.ds(c, n_heads, stride=2), :].astype(BF16).astype(F32)
            s = jnp.sum(qs[c].astype(F32) * kn, axis=-1, keepdims=True) * scale
            m_old = m_sc[c]
            m_fin = jnp.maximum(m_old, s)
            a = jnp.exp(m_old - m_fin)
            p = jnp.exp(s - m_fin)
            l_fin = a * l_sc[c] + p
            p_r = p.astype(BF16).astype(F32)
            outs.append((a * acc_sc[c] + p_r * vn) / l_fin)
        lam = _diff_lambda(lq1_ref, lk1_ref, lq2_ref, lk2_ref, lam_init)
        y = _subln(outs[0] - lam * outs[1], g_ref[...], lam_init)
        for h in range(n_heads):
            rows_sc[pl.ds(b, 1), h * hw:(h + 1) * hw] = y[h:h + 1, :]

    @pl.when((b == last_b) & (j == last_j))
    def _():
        o_ref[...] = rows_sc[...].astype(o_ref.dtype)


def _attn_sample(q_s, k_s, v_s, cache_k, cache_v, page_table, lams, subln_g, layer, *, ts, n_heads, hd):
    bs, n_pages = page_table.shape
    page = cache_k.shape[2]
    hw = 2 * hd
    pps = _pick_tile(n_pages, SAMPLE_PAGES_PER_STEP, 1)

    def page_spec(i, width_dims):
        return pl.BlockSpec((None, None, page) + width_dims,
                            lambda b, j, pt: (pt[b, j * pps + i], layer, 0, 0, 0))

    in_specs = [pl.BlockSpec((None, 2 * n_heads, hd), lambda b, j, pt: (b, 0, 0)),
                pl.BlockSpec((None, 2 * n_heads, hd), lambda b, j, pt: (b, 0, 0)),
                pl.BlockSpec((None, n_heads, hw), lambda b, j, pt: (b, 0, 0)),
                *_lambda_specs(2, layer, hd),
                pl.BlockSpec((None, 1, hw), lambda b, j, pt: (layer, 0, 0))]
    in_specs += [page_spec(i, (2 * n_heads, hd)) for i in range(pps)]
    in_specs += [page_spec(i, (n_heads, hw)) for i in range(pps)]
    ins = [page_table, q_s, k_s, v_s, *lams, subln_g] + [cache_k] * pps + [cache_v] * pps
    grid_spec = pltpu.PrefetchScalarGridSpec(
        num_scalar_prefetch=1, grid=(bs, n_pages // pps), in_specs=in_specs,
        out_specs=pl.BlockSpec((ts, n_heads * hw), lambda b, j, pt: (0, 0)),
        scratch_shapes=[pltpu.VMEM((2, n_heads, 1), F32), pltpu.VMEM((2, n_heads, 1), F32),
                        pltpu.VMEM((2, n_heads, hw), F32), pltpu.VMEM((ts, n_heads * hw), F32)])
    return pl.pallas_call(
        functools.partial(_attn_sample_kernel, n_pages_step=pps, n_heads=n_heads, hd=hd, scale=hd ** -0.5,
                          lam_init=_lambda_init(layer)),
        grid_spec=grid_spec, out_shape=jax.ShapeDtypeStruct((ts, n_heads * hw), BF16),
        compiler_params=_cparams(2), name="attn_sample")(*ins)


def _shift_rows(x, k, t_idx):
    return jnp.where(t_idx >= k, pltpu.roll(x, k, 0), 0.0)


def _pool_prompt_kernel(u_ref, w_ref, sc_ref, o_ref, *, gw):
    t = u_ref.shape[0]
    t_idx = lax.broadcasted_iota(jnp.int32, (t, 1), 0)
    for g, win in enumerate(POOL_WINDOWS):
        cols = slice(g * gw, (g + 1) * gw)
        u = u_ref[:, cols]
        s = u
        k = 1
        while k < win:
            s = s + _shift_rows(s, k, t_idx)
            k *= 2
        cnt = jnp.minimum(t_idx + 1, win).astype(F32)
        pooled = (s / cnt - u).astype(BF16)
        mixed = jnp.dot(pooled, w_ref[g].astype(BF16), preferred_element_type=F32)
        o_ref[:, cols] = (mixed * sc_ref[:, cols]).astype(o_ref.dtype)


def _pool_prompt(z, pool_w, pool_scale, layer, *, bp, t, off_pool):
    n_g, gw = pool_w.shape[1], pool_w.shape[2]
    pw = n_g * gw
    return pl.pallas_call(
        functools.partial(_pool_prompt_kernel, gw=gw),
        grid=(bp,),
        in_specs=[pl.BlockSpec((t, pw), lambda b: (b, off_pool // pw)),
                  pl.BlockSpec((None, n_g, gw, gw), lambda b: (layer, 0, 0, 0)),
                  pl.BlockSpec((None, 1, pw), lambda b: (layer, 0, 0))],
        out_specs=pl.BlockSpec((t, pw), lambda b: (b, 0)),
        out_shape=jax.ShapeDtypeStruct((bp * t, pw), BF16),
        compiler_params=_cparams(1), name="pool_prompt")(z, pool_w, pool_scale)


def _conv_prompt_kernel(cb_ref, cc_ref, cx_ref, w_ref, o_ref, tail_ref):
    t = cb_ref.shape[0]
    t_idx = lax.broadcasted_iota(jnp.int32, (t, 1), 0)
    u = cc_ref[...] * cx_ref[...]
    w = w_ref[...]
    zc = w[0:1] * _shift_rows(u, 2, t_idx) + w[1:2] * _shift_rows(u, 1, t_idx) + w[2:3] * u
    o_ref[...] = (cb_ref[...] * zc).astype(o_ref.dtype)
    tail_ref[...] = u[t - 8:, :]


def _conv_prompt(z, conv_w, layer, *, bp, t, off_cb, off_cc, off_cx):
    cw = conv_w.shape[2]
    cbk = _pick_tile(cw, 256, V7X_LANES)
    ncb = cw // cbk

    def zspec(off):
        return pl.BlockSpec((t, cbk), lambda b, c: (b, off // cbk + c))

    return pl.pallas_call(
        _conv_prompt_kernel, grid=(bp, ncb),
        in_specs=[zspec(off_cb), zspec(off_cc), zspec(off_cx),
                  pl.BlockSpec((None, conv_w.shape[1], cbk), lambda b, c: (layer, 0, c))],
        out_specs=[pl.BlockSpec((t, cbk), lambda b, c: (b, c)),
                   pl.BlockSpec((None, 8, cbk), lambda b, c: (b, 0, c))],
        out_shape=[jax.ShapeDtypeStruct((bp * t, cw), BF16), jax.ShapeDtypeStruct((bp, 8, cw), F32)],
        compiler_params=_cparams(2), name="conv_prompt")(z, z, z, conv_w)


def _mix_sample_kernel(pu_ref, cb_ref, cc_ref, cx_ref, hp_ref, hc_ref, pw_ref, ps_ref, cw_ref,
                       pm_ref, cm_ref, cu_ref, *, bs, gw, past_len):
    n_hist = hp_ref.shape[0]
    pm_ref[...] = jnp.zeros(pm_ref.shape, pm_ref.dtype)
    cm_ref[...] = jnp.zeros(cm_ref.shape, cm_ref.dtype)
    for g, win in enumerate(POOL_WINDOWS):
        cols = slice(g * gw, (g + 1) * gw)
        u = pu_ref[:, cols]
        s = u
        for i in range(1, win):
            s = s + hp_ref[n_hist - i][:, cols]
        cnt = float(min(past_len + 1, win))
        pooled = (s / cnt - u).astype(BF16)
        mixed = jnp.dot(pooled, pw_ref[g].astype(BF16), preferred_element_type=F32)
        pm_ref[0:bs, cols] = (mixed * ps_ref[:, cols]).astype(pm_ref.dtype)
    u = cc_ref[...] * cx_ref[...]
    w = cw_ref[...]
    zc = w[0:1] * hc_ref[0] + w[1:2] * hc_ref[1] + w[2:3] * u
    cm_ref[0:bs, :] = (cb_ref[...] * zc).astype(cm_ref.dtype)
    cu_ref[...] = u


def _mix_sample(z, hist_pool, hist_conv, pool_w, pool_scale, conv_w, layer, *,
                row0, ts, bs, off_pool, off_cb, off_cc, off_cx, past_len):
    n_g, gw = pool_w.shape[1], pool_w.shape[2]
    pw = n_g * gw
    cw = conv_w.shape[2]

    def zspec(off, width):
        return pl.BlockSpec((bs, width), lambda i: (row0 // bs, off // width))

    return pl.pallas_call(
        functools.partial(_mix_sample_kernel, bs=bs, gw=gw, past_len=past_len),
        grid=(1,),
        in_specs=[zspec(off_pool, pw), zspec(off_cb, cw), zspec(off_cc, cw), zspec(off_cx, cw),
                  pl.BlockSpec(hist_pool.shape, lambda i: (0, 0, 0)),
                  pl.BlockSpec(hist_conv.shape, lambda i: (0, 0, 0)),
                  pl.BlockSpec((None, n_g, gw, gw), lambda i: (layer, 0, 0, 0)),
                  pl.BlockSpec((None, 1, pw), lambda i: (layer, 0, 0)),
                  pl.BlockSpec((None, conv_w.shape[1], cw), lambda i: (layer, 0, 0))],
        out_specs=[pl.BlockSpec((ts, pw), lambda i: (0, 0)),
                   pl.BlockSpec((ts, cw), lambda i: (0, 0)),
                   pl.BlockSpec((bs, cw), lambda i: (0, 0))],
        out_shape=[jax.ShapeDtypeStruct((ts, pw), BF16), jax.ShapeDtypeStruct((ts, cw), BF16),
                   jax.ShapeDtypeStruct((bs, cw), F32)],
        compiler_params=_cparams(1), name="mix_sample")(
            z, z, z, z, hist_pool, hist_conv, pool_w, pool_scale, conv_w)


def _merge_kernel(pm_ref, oa_ref, cm_ref, g0_ref, g1_ref, g2_ref, wp_ref, wa_ref, wc_ref, o_ref,
                  wp_sc, wa_sc, wc_sc):
    @pl.when(pl.program_id(1) == 0)
    def _():
        wp_sc[...] = wp_ref[...].astype(BF16)
        wa_sc[...] = wa_ref[...].astype(BF16)
        wc_sc[...] = wc_ref[...].astype(BF16)

    br_pool = jnp.dot(pm_ref[...], wp_sc[...], preferred_element_type=F32)
    br_attn = jnp.dot(oa_ref[...], wa_sc[...], preferred_element_type=F32)
    br_conv = jnp.dot(cm_ref[...], wc_sc[...], preferred_element_type=F32)
    merged = (jax.nn.sigmoid(g0_ref[...]) * br_pool + jax.nn.sigmoid(g1_ref[...]) * br_attn
              + jax.nn.sigmoid(g2_ref[...]) * br_conv)
    o_ref[...] = merged.astype(o_ref.dtype)


def _merge(pm, oa, cm, z, w_up_pool, w_up_attn, w_up_conv, layer, *, off_g, d, tm, tn):
    m = pm.shape[0]
    pw, aw, cw = pm.shape[1], oa.shape[1], cm.shape[1]

    def act(width):
        return pl.BlockSpec((tm, width), lambda j, i: (i, 0))

    def gate(k):
        return pl.BlockSpec((tm, tn), lambda j, i: (i, (off_g + k * d) // tn + j))

    def wspec(width):
        return pl.BlockSpec((None, width, tn), lambda j, i: (layer, 0, j))

    return pl.pallas_call(
        _merge_kernel, grid=(d // tn, m // tm),
        in_specs=[act(pw), act(aw), act(cw), gate(0), gate(1), gate(2), wspec(pw), wspec(aw), wspec(cw)],
        out_specs=pl.BlockSpec((tm, tn), lambda j, i: (i, j)),
        out_shape=jax.ShapeDtypeStruct((m, d), BF16),
        scratch_shapes=[pltpu.VMEM((pw, tn), BF16), pltpu.VMEM((aw, tn), BF16), pltpu.VMEM((cw, tn), BF16)],
        compiler_params=_cparams(2), name="merge")(pm, oa, cm, z, z, z, w_up_pool, w_up_attn, w_up_conv)


def _swiglu_step(h, w1_ref, w3_ref, w2_ref):
    a = jnp.dot(h, w1_ref[...].astype(BF16), preferred_element_type=F32)
    b = jnp.dot(h, w3_ref[...].astype(BF16), preferred_element_type=F32)
    act = (a * jax.nn.sigmoid(a)) * b
    return jnp.dot(act.astype(BF16), w2_ref[...].astype(BF16), preferred_element_type=F32)


def _ffn_kernel(h_ref, w1_ref, w3_ref, w2_ref, o_ref):
    @pl.when(pl.program_id(1) == 0)
    def _():
        o_ref[...] = jnp.zeros(o_ref.shape, o_ref.dtype)

    o_ref[...] += _swiglu_step(h_ref[...], w1_ref, w3_ref, w2_ref)


def _ffn(h, w1, w3, w2, idx, *, tm, tf):
    m, d = h.shape
    d_ff = w1.shape[2]
    up = pl.BlockSpec((None, d, tf), lambda i, f: (idx, 0, f))
    down = pl.BlockSpec((None, tf, d), lambda i, f: (idx, f, 0))
    return pl.pallas_call(
        _ffn_kernel, grid=(m // tm, d_ff // tf),
        in_specs=[pl.BlockSpec((tm, d), lambda i, f: (i, 0)), up, up, down],
        out_specs=pl.BlockSpec((tm, d), lambda i, f: (i, 0)),
        out_shape=jax.ShapeDtypeStruct((m, d), F32),
        compiler_params=_cparams(2), name="ffn")(h, w1, w3, w2)


def _cumsum_rows(counts):
    n, e = counts.shape
    blk = _pick_tile(n, V7X_LANES, 1)
    c = counts.reshape(n // blk, blk, e)
    tri = (jnp.arange(blk)[:, None] >= jnp.arange(blk)[None, :]).astype(F32)
    within = jnp.einsum("ij,bje->bie", tri, c.astype(F32)).astype(jnp.int32)
    before = jnp.cumsum(within[:, -1, :], axis=0) - within[:, -1, :]
    return (within + before[:, None, :]).reshape(n, e)


def _route_tables(route, n_valid, n_experts, tile_rows):
    m = route.shape[0]
    idx = route[:, :TOP_K].astype(jnp.int32)
    valid = jnp.arange(m) < n_valid
    wts = jnp.where(valid[:, None], route[:, TOP_K:2 * TOP_K], 0.0)
    onehot = ((idx[:, :, None] == jnp.arange(n_experts)) & valid[:, None, None]).astype(jnp.int32)
    onehot = onehot.reshape(m * TOP_K, n_experts)
    csum = _cumsum_rows(onehot)
    rank = jnp.sum((csum - onehot) * onehot, axis=1)
    tiles_e = (csum[-1] + tile_rows - 1) // tile_rows
    tile_end = jnp.cumsum(tiles_e)
    start_row = (tile_end - tiles_e) * tile_rows
    valid2 = jnp.repeat(valid, TOP_K)
    pos = jnp.where(valid2, start_row[idx.reshape(-1)] + rank, 0)
    n_tiles = (n_valid * TOP_K) // tile_rows + n_experts
    n_rows = n_tiles * tile_rows
    tok = jnp.repeat(jnp.arange(m, dtype=jnp.int32), TOP_K)
    src = jnp.zeros((n_rows,), jnp.int32).at[jnp.where(valid2, pos, n_rows)].set(tok, mode="drop")
    n_used = tile_end[-1]
    tile_id = jnp.minimum(jnp.arange(n_tiles), n_used - 1)
    tile_expert = jnp.sum((tile_end[None, :] <= tile_id[:, None]).astype(jnp.int32), axis=1)
    tile_expert = jnp.clip(tile_expert, 0, n_experts - 1).astype(jnp.int32)
    tile_in_group = tile_id - (tile_end - tiles_e)[tile_expert]
    real_rows = jnp.clip(csum[-1][tile_expert] - tile_in_group * tile_rows, 0, tile_rows)
    row_block = _moe_row_block(tile_rows)
    n_blocks = (real_rows + row_block - 1) // row_block
    tile_meta = jnp.concatenate([n_used.reshape(1), n_blocks]).astype(jnp.int32)
    return src.reshape(n_tiles, 1, tile_rows), tile_expert, tile_meta, pos.astype(jnp.int32), wts


def _moe_row_block(tile_rows):
    return _pick_tile(tile_rows, 256, 16)


def _row_copy(src_hbm, src_row, dst_vmem, dst_row, sem):
    return pltpu.make_async_copy(src_hbm.at[pl.ds(src_row, 1), :], dst_vmem.at[pl.ds(dst_row, 1), :], sem)


def _moe_group_kernel(te_ref, nu_ref, src_ref, h_hbm, w1_ref, w3_ref, w2_ref, o_ref, x32_sc, x16_sc, sem,
                      *, row_block):
    del te_ref
    i = pl.program_id(0)
    f = pl.program_id(1)
    used = i < nu_ref[0]
    rows = x32_sc.shape[0]

    @pl.when(f == 0)
    def _():
        o_ref[...] = jnp.zeros(o_ref.shape, o_ref.dtype)

    @pl.when(used & (f == 0))
    def _():
        def issue(r, carry):
            _row_copy(h_hbm, src_ref[0, r], x32_sc, r, sem).start()
            return carry

        def wait(r, carry):
            _row_copy(h_hbm, 0, x32_sc, r, sem).wait()
            return carry

        lax.fori_loop(0, rows, issue, 0)
        lax.fori_loop(0, rows, wait, 0)
        x16_sc[...] = x32_sc[...].astype(BF16)

    n_blocks = nu_ref[1 + i]
    for k in range(1, rows // row_block + 1):
        @pl.when(used & (n_blocks == k))
        def _(mk=k * row_block):
            o_ref[0:mk, :] += _swiglu_step(x16_sc[0:mk, :], w1_ref, w3_ref, w2_ref)


def _moe_group(h32, src, tile_expert, n_used, w1, w3, w2, idx, *, tf):
    n_tiles, _, rows = src.shape
    d = h32.shape[1]
    d_ff = w1.shape[3]
    nf = d_ff // tf

    def f_blk(i, f, nu):
        return jnp.where(i < nu[0], f, nf - 1)

    up = pl.BlockSpec((None, None, d, tf), lambda i, f, te, nu: (idx, te[i], 0, f_blk(i, f, nu)))
    down = pl.BlockSpec((None, None, tf, d), lambda i, f, te, nu: (idx, te[i], f_blk(i, f, nu), 0))
    grid_spec = pltpu.PrefetchScalarGridSpec(
        num_scalar_prefetch=2, grid=(n_tiles, nf),
        in_specs=[pl.BlockSpec((None, 1, rows), lambda i, f, te, nu: (i, 0, 0), memory_space=pltpu.SMEM),
                  pl.BlockSpec(memory_space=pl.ANY), up, up, down],
        out_specs=pl.BlockSpec((rows, d), lambda i, f, te, nu: (i, 0)),
        scratch_shapes=[pltpu.VMEM((rows, d), F32), pltpu.VMEM((rows, d), BF16), pltpu.SemaphoreType.DMA(())])
    return pl.pallas_call(
        functools.partial(_moe_group_kernel, row_block=_moe_row_block(rows)), grid_spec=grid_spec,
        out_shape=jax.ShapeDtypeStruct((n_tiles * rows, d), F32),
        compiler_params=_cparams(2), name="moe_group")(tile_expert, n_used, src, h32, w1, w3, w2)


def _moe_combine_kernel(pos_ref, w_ref, y_hbm, o_ref, y_sc, sem):
    rows = o_ref.shape[0]

    def issue(r, carry):
        for s in range(TOP_K):
            _row_copy(y_hbm, pos_ref[0, TOP_K * r + s], y_sc.at[s], r, sem).start()
        return carry

    def wait(r, carry):
        for s in range(TOP_K):
            _row_copy(y_hbm, 0, y_sc.at[s], r, sem).wait()
        return carry

    lax.fori_loop(0, rows, issue, 0)
    lax.fori_loop(0, rows, wait, 0)
    w = w_ref[...]
    acc = w[:, 0:1] * y_sc[0]
    for s in range(1, TOP_K):
        acc = acc + w[:, s:s + 1] * y_sc[s]
    o_ref[...] = acc


def _moe_combine(y, pos, wts, *, tm):
    m = wts.shape[0]
    d = y.shape[1]
    return pl.pallas_call(
        _moe_combine_kernel, grid=(m // tm,),
        in_specs=[pl.BlockSpec((None, 1, TOP_K * tm), lambda i: (i, 0, 0), memory_space=pltpu.SMEM),
                  pl.BlockSpec((tm, TOP_K), lambda i: (i, 0)),
                  pl.BlockSpec(memory_space=pl.ANY)],
        out_specs=pl.BlockSpec((tm, d), lambda i: (i, 0)),
        out_shape=jax.ShapeDtypeStruct((m, d), F32),
        scratch_shapes=[pltpu.VMEM((TOP_K, tm, d), F32), pltpu.SemaphoreType.DMA(())],
        compiler_params=_cparams(1), name="moe_combine")(pos.reshape(m // tm, 1, TOP_K * tm), wts, y)


def _moe(h32, route, w1, w3, w2, idx, *, n_valid, tile_rows, tf, tm_combine):
    src, tile_expert, n_used, pos, wts = _route_tables(route, n_valid, w1.shape[1], tile_rows)
    y = _moe_group(h32, src, tile_expert, n_used, w1, w3, w2, idx, tf=tf)
    return _moe_combine(y, pos, wts, tm=tm_combine)


def kernel(x_prompt, x_sample, cache_k, cache_v, state_pool, state_conv, page_table, meta_tokens, norm_mix, w_in, lambda_q1, lambda_k1, lambda_q2, lambda_k2, subln_g, pool_w, pool_scale, conv_w, w_up_pool, w_up_attn, w_up_conv, w_o, norm_ffn, ffn_w1, ffn_w3, ffn_w2, router_w, moe_w1, moe_w3, moe_w2, norm_final):
    bp, seq, d = x_prompt.shape
    bs, dec_seq, _ = x_sample.shape
    assert dec_seq == 1 and bs % 8 == 0
    depth = w_in.shape[0]
    n_meta = meta_tokens.shape[0]
    t = seq + n_meta
    hd = cache_k.shape[-1]
    n_heads = cache_v.shape[-2]
    hw = 2 * hd
    att_w = n_heads * hw
    pw = state_pool.shape[-1]
    cw = state_conv.shape[-1]
    page = cache_k.shape[2]
    past_len = page_table.shape[1] * page
    off_q, off_k, off_v = 0, att_w, 2 * att_w
    off_pool = 3 * att_w
    off_cb = off_pool + pw
    off_cc, off_cx, off_g = off_cb + cw, off_cb + 2 * cw, off_cb + 3 * cw
    in_w = off_g + 3 * d
    assert w_in.shape[2] == in_w and state_pool.shape[2] == max(POOL_WINDOWS) - 1

    n_prompt = bp * t
    ts = next(c for c in (64, 32, 16, 8) if n_prompt % c == 0 and c >= bs)
    m_pad = n_prompt + ts
    tm = _pick_tile(m_pad, ROW_TILE_TARGET)
    tq = _pick_tile(t, SEQ_TILE_TARGET)
    tn_in = _pick_tile(in_w, 1024, V7X_LANES)
    tn_d = _pick_tile(d, 512, V7X_LANES)
    tf = _pick_tile(ffn_w1.shape[2], 256, V7X_LANES)
    tm_merge = _pick_tile(m_pad, ROW_TILE_TARGET // 2)
    tm_norm = _pick_tile(m_pad, NORM_TILE_TARGET)
    assert n_prompt % bs == 0 and all(o % w == 0 for o, w in
                                      ((off_k, hw), (off_v, hw), (off_pool, pw), (off_cb, cw), (off_g, tn_d)))

    dt = x_prompt.dtype
    meta = jnp.broadcast_to(meta_tokens.astype(dt)[None], (bp, n_meta, d))
    x = jnp.concatenate([jnp.concatenate([meta, x_prompt], axis=1).reshape(n_prompt, d),
                         x_sample.reshape(bs, d), jnp.zeros((ts - bs, d), dt)], axis=0)

    lam_vecs = [v.reshape(depth, 1, hd) for v in (lambda_q1, lambda_k1, lambda_q2, lambda_k2)]
    subln = subln_g.reshape(depth, 1, hw)
    pool_sc = pool_scale.reshape(depth, 1, pw)

    kp_l, vp_l, pp_l, cp_l, ks_l, vs_l, ps_l, cs_l = ([] for _ in range(8))
    f = None
    for l in range(depth):
        x, h, _ = _norm(x, f, norm_mix[l], None, tm=tm_norm, out_dtype=BF16)
        z = _matmul(h, w_in, l, None, tm=tm, tn=tn_in, out_dtype=F32)

        zs = z[n_prompt:n_prompt + bs]
        q_s = zs[:, off_q:off_k].reshape(bs, 2 * n_heads, hd)
        k_s = zs[:, off_k:off_v].reshape(bs, 2 * n_heads, hd)
        v_s = zs[:, off_v:off_pool].reshape(bs, n_heads, hw)

        def prompt_cols(c0, c1):
            return lax.slice(z, (0, c0), (n_prompt, c1)).reshape(bp, t, c1 - c0)

        oa_p = _attn_prompt(z, lam_vecs, subln, l, bp=bp, t=t, n_heads=n_heads, hd=hd,
                            off_q=off_q, off_k=off_k, off_v=off_v, tq=tq)
        oa_s = _attn_sample(q_s, k_s, v_s, cache_k, cache_v, page_table, lam_vecs, subln, l,
                            ts=ts, n_heads=n_heads, hd=hd)
        pm_p = _pool_prompt(z, pool_w, pool_sc, l, bp=bp, t=t, off_pool=off_pool)
        cm_p, conv_tail = _conv_prompt(z, conv_w, l, bp=bp, t=t, off_cb=off_cb, off_cc=off_cc, off_cx=off_cx)
        pm_s, cm_s, cu_s = _mix_sample(z, jnp.swapaxes(state_pool[l], 0, 1), jnp.swapaxes(state_conv[l], 0, 1),
                                       pool_w, pool_sc, conv_w, l, row0=n_prompt, ts=ts, bs=bs, off_pool=off_pool,
                                       off_cb=off_cb, off_cc=off_cc, off_cx=off_cx, past_len=past_len)
        oa = jnp.concatenate([oa_p, oa_s], axis=0)
        pm = jnp.concatenate([pm_p, pm_s], axis=0)
        cm = jnp.concatenate([cm_p, cm_s], axis=0)
        merged = _merge(pm, oa, cm, z, w_up_pool, w_up_attn, w_up_conv, l, off_g=off_g, d=d,
                        tm=tm_merge, tn=tn_d)
        x = _matmul(merged, w_o, l, x, tm=tm, tn=tn_d, out_dtype=F32)

        if l % 2 == 0:
            _, h2, _ = _norm(x, None, norm_ffn[l], None, tm=tm_norm, out_dtype=BF16)
            f = _ffn(h2, ffn_w1, ffn_w3, ffn_w2, l // 2, tm=tm, tf=tf)
        else:
            _, h2, route = _norm(x, None, norm_ffn[l], router_w[l // 2], tm=tm_norm, out_dtype=F32)
            f = _moe(h2, route, moe_w1, moe_w3, moe_w2, l // 2, n_valid=n_prompt + bs, tile_rows=tm, tf=tf,
                     tm_combine=tm_merge)

        kp_l.append(prompt_cols(off_k, off_v).reshape(bp, t, 2 * n_heads, hd))
        vp_l.append(prompt_cols(off_v, off_pool).reshape(bp, t, n_heads, hw))
        pp_l.append(prompt_cols(off_pool, off_cb)[:, t - state_pool.shape[2]:])
        cp_l.append(conv_tail[:, 8 - state_conv.shape[2]:])
        ks_l.append(k_s.reshape(bs, 1, 2 * n_heads, hd))
        vs_l.append(v_s.reshape(bs, 1, n_heads, hw))
        ps_l.append(jnp.concatenate([state_pool[l][:, 1:], zs[:, None, off_pool:off_cb]], axis=1))
        cs_l.append(jnp.concatenate([state_conv[l][:, 1:], cu_s[:, None]], axis=1))

    _, y, _ = _norm(x, f, norm_final, None, tm=tm_norm, out_dtype=F32)
    y_prompt = y[:n_prompt].reshape(bp, t, d)[:, n_meta:]
    y_sample = y[n_prompt:n_prompt + bs].reshape(bs, 1, d)
    return (y_prompt, y_sample,
            jnp.stack(kp_l, axis=1), jnp.stack(vp_l, axis=1), jnp.stack(pp_l, axis=0), jnp.stack(cp_l, axis=0),
            jnp.stack(ks_l, axis=1), jnp.stack(vs_l, axis=1), jnp.stack(ps_l, axis=0), jnp.stack(cs_l, axis=0))
```
